```python
import math
import jax, jax.numpy as jnp
from jax import lax
import numpy as np

D_MODEL = 1024
BATCH = 8
SEQ = 8192
DEPTH = 4

MEM_LEN = 256
Q_BLOCK = 128
NSA_Q_BLOCK = 64
NEG_INF = -1e30
EPS = 1e-6
SEL_FORCE = 1e9

NUM_BUCKETS = 32
REL_MAX_DIST = 128
BIAS_HEADS = 8

MLA_HEADS = 8
MLA_Q_RANK = 256
MLA_KV_RANK = 128
MLA_NOPE = 64
MLA_ROPE = 32
MLA_V = 64
ROPE_BASE = 10000.0

SWA_HEADS = 8
SWA_KV_HEADS = 2
SWA_HEAD_DIM = 64
SWA_WINDOW = 128

NSA_HEADS = 8
NSA_KV_HEADS = 2
NSA_HEAD_DIM = 64
CMP_LEN = 32
CMP_STRIDE = 16
CMP_HIDDEN = 256
SEL_LEN = 64
SEL_TOP_N = 16
NSA_WINDOW = 512

MEM_HEADS = 4
MEM_HEAD_DIM = 128

MLA_W = MLA_HEADS * MLA_V
SWA_W = SWA_HEADS * SWA_HEAD_DIM
SWA_KV_W = SWA_KV_HEADS * SWA_HEAD_DIM
NSA_W = NSA_HEADS * NSA_HEAD_DIM
NSA_KV_W = NSA_KV_HEADS * NSA_HEAD_DIM
MEM_W = MEM_HEADS * MEM_HEAD_DIM

EVEN_SPLITS = (MLA_Q_RANK, MLA_KV_RANK, MLA_ROPE, MLA_W,
               SWA_W, SWA_KV_W, SWA_KV_W, SWA_W,
               MEM_W, MEM_W)
ODD_SPLITS = (NSA_W,
              NSA_KV_W, NSA_KV_W,
              NSA_KV_W, NSA_KV_W,
              NSA_KV_W, NSA_KV_W,
              NSA_HEADS * 3,
              NSA_W,
              MEM_W, MEM_W)
EVEN_IN = sum(EVEN_SPLITS)
ODD_IN = sum(ODD_SPLITS)
EVEN_OUT = MLA_W + SWA_W + MEM_W
ODD_OUT = NSA_W + MEM_W
N_EVEN = (DEPTH + 1) // 2
N_ODD = DEPTH // 2

kernel_name = "hybrid_mla_swa_nsa_memory_trunk"


def split_cols(x, sizes):
    cuts = [int(c) for c in np.cumsum(sizes)[:-1]]
    return jnp.split(x, cuts, axis=-1)


def rms_norm(x, g):
    xf = x.astype(jnp.float32)
    y = xf * lax.rsqrt(jnp.mean(jnp.square(xf), axis=-1, keepdims=True) + EPS)
    return (y * g.astype(jnp.float32)).astype(x.dtype)


def rope(x, pos):
    half = x.shape[-1] // 2
    inv = ROPE_BASE ** (-jnp.arange(half, dtype=jnp.float32) / half)
    ang = pos.astype(jnp.float32)[:, None] * inv[None, :]
    cos = jnp.cos(ang)[None, :, None, :]
    sin = jnp.sin(ang)[None, :, None, :]
    x1 = x[..., :half].astype(jnp.float32)
    x2 = x[..., half:].astype(jnp.float32)
    return jnp.concatenate([x1 * cos - x2 * sin, x2 * cos + x1 * sin], axis=-1).astype(x.dtype)


def t5_bucket(dist):
    max_exact = NUM_BUCKETS // 2
    d = jnp.maximum(dist, 0)
    log_ratio = jnp.log(jnp.maximum(d, 1).astype(jnp.float32) / max_exact) / math.log(REL_MAX_DIST / max_exact)
    large = max_exact + (log_ratio * (NUM_BUCKETS - max_exact)).astype(jnp.int32)
    return jnp.where(d < max_exact, d, jnp.minimum(large, NUM_BUCKETS - 1))


def rel_bias(table, dist, n_groups):
    b = table.astype(jnp.float32)[t5_bucket(dist)]
    q, k, h = b.shape
    return jnp.transpose(b, (2, 0, 1)).reshape(n_groups, h // n_groups, q, k)


def masked_softmax(logits, mask, sink=None):
    logits = jnp.where(mask, logits, NEG_INF)
    m = jnp.max(logits, axis=-1, keepdims=True)
    if sink is not None:
        m = jnp.maximum(m, sink)
    p = jnp.where(mask, jnp.exp(logits - m), 0.0)
    denom = jnp.sum(p, axis=-1, keepdims=True)
    if sink is not None:
        denom = denom + jnp.exp(sink - m)
    return p / jnp.maximum(denom, 1e-30)


def sweep(fn, n_blocks):
    out = lax.map(fn, jnp.arange(n_blocks))
    out = jnp.moveaxis(out, 0, 1)
    return out.reshape(out.shape[0], -1, *out.shape[3:])


def mla_attention(c_q, c_kv, k_rope, pos, q_norm, kv_norm, w_uq, w_ukv):
    B, S, _ = c_q.shape
    q = (rms_norm(c_q, q_norm) @ w_uq).reshape(B, S, MLA_HEADS, MLA_NOPE + MLA_ROPE)
    q_nope = q[..., :MLA_NOPE]
    q_pe = rope(q[..., MLA_NOPE:], pos)
    kv = (rms_norm(c_kv, kv_norm) @ w_ukv).reshape(B, S, MLA_HEADS, MLA_NOPE + MLA_V)
    k_nope, v = kv[..., :MLA_NOPE], kv[..., MLA_NOPE:]
    k_pe = rope(k_rope[:, :, None, :], pos)[:, :, 0, :]
    scale = (MLA_NOPE + MLA_ROPE) ** -0.5
    k_idx = jnp.arange(S)

    def block(i):
        s0 = i * Q_BLOCK
        qn = lax.dynamic_slice_in_dim(q_nope, s0, Q_BLOCK, axis=1)
        qp = lax.dynamic_slice_in_dim(q_pe, s0, Q_BLOCK, axis=1)
        logits = (jnp.einsum('bqhd,bkhd->bhqk', qn, k_nope)
                  + jnp.einsum('bqhd,bkd->bhqk', qp, k_pe)).astype(jnp.float32) * scale
        mask = (s0 + jnp.arange(Q_BLOCK))[:, None] >= k_idx[None, :]
        p = masked_softmax(logits, mask)
        return jnp.einsum('bhqk,bkhd->bqhd', p.astype(v.dtype), v).reshape(B, Q_BLOCK, MLA_W)

    return sweep(block, S // Q_BLOCK)


def banded_attention(q, k, v, pos, table, window, sinks=None):
    B, S, H, d = q.shape
    G = k.shape[2]
    R = H // G
    pad = -(-(window - 1) // Q_BLOCK) * Q_BLOCK
    kl = pad + Q_BLOCK
    kp = jnp.pad(k, ((0, 0), (pad, 0), (0, 0), (0, 0)))
    vp = jnp.pad(v, ((0, 0), (pad, 0), (0, 0), (0, 0)))
    posp = jnp.pad(pos, (pad, 0))
    scale = d ** -0.5
    sink = None if sinks is None else sinks.astype(jnp.float32).reshape(G, R)[None, :, :, None, None]

    def block(i):
        s0 = i * Q_BLOCK
        qb = lax.dynamic_slice_in_dim(q, s0, Q_BLOCK, axis=1).reshape(B, Q_BLOCK, G, R, d)
        kb = lax.dynamic_slice_in_dim(kp, s0, kl, axis=1)
        vb = lax.dynamic_slice_in_dim(vp, s0, kl, axis=1)
        q_idx = s0 + jnp.arange(Q_BLOCK)
        k_idx = s0 - pad + jnp.arange(kl)
        delta = q_idx[:, None] - k_idx[None, :]
        mask = (k_idx[None, :] >= 0) & (delta >= 0) & (delta < window)
        dist = (lax.dynamic_slice_in_dim(pos, s0, Q_BLOCK)[:, None]
                - lax.dynamic_slice_in_dim(posp, s0, kl)[None, :])
        logits = (jnp.einsum('bqgrd,bkgd->bgrqk', qb, kb).astype(jnp.float32) * scale
                  + rel_bias(table, dist, G))
        p = masked_softmax(logits, mask, sink)
        return jnp.einsum('bgrqk,bkgd->bqgrd', p.astype(v.dtype), vb).reshape(B, Q_BLOCK, H * d)

    return sweep(block, S // Q_BLOCK)


def memory_attention(q, mem_n, w_mem_kv):
    B, S, _ = q.shape
    M = mem_n.shape[1]
    kv = mem_n @ w_mem_kv
    k = kv[..., :MEM_W].reshape(B, M, MEM_HEADS, MEM_HEAD_DIM)
    v = kv[..., MEM_W:].reshape(B, M, MEM_HEADS, MEM_HEAD_DIM)
    qh = q.reshape(B, S, MEM_HEADS, MEM_HEAD_DIM)
    logits = jnp.einsum('bshd,bmhd->bhsm', qh, k).astype(jnp.float32) * MEM_HEAD_DIM ** -0.5
    p = jax.nn.softmax(logits, axis=-1)
    return jnp.einsum('bhsm,bmhd->bshd', p.astype(v.dtype), v).reshape(B, S, MEM_W)


def compress_blocks(x, pos_emb, w1, w2):
    B, S, G, d = x.shape
    nc = (S - CMP_LEN) // CMP_STRIDE + 1
    idx = jnp.arange(nc)[:, None] * CMP_STRIDE + jnp.arange(CMP_LEN)[None, :]
    blk = x[:, idx] + pos_emb[None, None, :, None, :]
    flat = jnp.transpose(blk, (0, 1, 3, 2, 4)).reshape(B, nc, G, CMP_LEN * d)
    return jax.nn.silu(flat @ w1) @ w2


def nsa_compressed_selected(q, k_cmp, v_cmp, k_slc, v_slc, pos, table,
                            cmp_pos_k, cmp_pos_v, w1k, w2k, w1v, w2v):
    B, S, H, d = q.shape
    G = k_slc.shape[2]
    R = H // G
    scale = d ** -0.5
    kc = compress_blocks(k_cmp, cmp_pos_k, w1k, w2k)
    vc = compress_blocks(v_cmp, cmp_pos_v, w1v, w2v)
    nc = kc.shape[1]
    c_start = jnp.arange(nc) * CMP_STRIDE
    c_end = c_start + CMP_LEN - 1
    c_pos = pos[c_end]
    ns = S // SEL_LEN
    top_n = min(SEL_TOP_N, ns)
    s_start = jnp.arange(ns) * SEL_LEN
    overlap = jnp.maximum(jnp.minimum(c_start[:, None] + CMP_LEN, s_start[None, :] + SEL_LEN)
                          - jnp.maximum(c_start[:, None], s_start[None, :]), 0)
    cmp_to_sel = overlap.astype(jnp.float32) / CMP_LEN
    ks = k_slc.reshape(B, ns, SEL_LEN, G, d).transpose(0, 3, 1, 2, 4)
    vs = v_slc.reshape(B, ns, SEL_LEN, G, d).transpose(0, 3, 1, 2, 4)
    gather = jax.vmap(jax.vmap(lambda blocks, ids: blocks[ids]))
    tbl_g = table.astype(jnp.float32).reshape(NUM_BUCKETS, G, R).transpose(1, 0, 2)
    QB = NSA_Q_BLOCK
    KS = top_n * SEL_LEN

    def block(i):
        s0 = i * QB
        qb = lax.dynamic_slice_in_dim(q, s0, QB, axis=1).reshape(B, QB, G, R, d)
        q_idx = s0 + jnp.arange(QB)
        q_pos = lax.dynamic_slice_in_dim(pos, s0, QB)
        logits = (jnp.einsum('bqgrd,bcgd->bgrqc', qb, kc).astype(jnp.float32) * scale
                  + rel_bias(table, q_pos[:, None] - c_pos[None, :], G))
        p_cmp = masked_softmax(logits, c_end[None, :] <= q_idx[:, None])
        o_cmp = jnp.einsum('bgrqc,bcgd->bqgrd', p_cmp.astype(vc.dtype), vc).reshape(B, QB, H * d)
        imp = jnp.einsum('bgqc,cj->bgqj', jnp.sum(p_cmp, axis=2), cmp_to_sel)
        j = jnp.arange(ns)[None, :]
        cur = (q_idx // SEL_LEN)[:, None]
        forced = (j == 0) | (j == cur) | (j == cur - 1)
        causal = j * SEL_LEN <= q_idx[:, None]
        score = jnp.where(forced, SEL_FORCE, jnp.where(causal, imp, -SEL_FORCE))
        _, sel = lax.top_k(score, top_n)
        kg = gather(ks, sel)
        vg = gather(vs, sel).reshape(B, G, QB, KS, d)
        tok = sel[..., None] * SEL_LEN + jnp.arange(SEL_LEN)
        tmask = (tok <= q_idx[None, None, :, None, None]).reshape(B, G, 1, QB, KS)
        bucket = t5_bucket(q_pos[None, None, :, None, None] - pos[tok])
        bias = tbl_g[jnp.arange(G)[None, :, None, None, None], bucket]
        bias = jnp.moveaxis(bias, -1, 2).reshape(B, G, R, QB, KS)
        logits = (jnp.einsum('bqgrd,bgqnld->bgrqnl', qb, kg).reshape(B, G, R, QB, KS).astype(jnp.float32)
                  * scale + bias)
        p_slc = masked_softmax(logits, tmask)
        o_slc = jnp.einsum('bgrqk,bgqkd->bqgrd', p_slc.astype(vg.dtype), vg).reshape(B, QB, H * d)
        return jnp.concatenate([o_cmp, o_slc], axis=-1)

    out = sweep(block, S // QB)
    return out[..., :H * d], out[..., H * d:]


def even_layer(h, mem_n, pos, table, w_in, q_norm, kv_norm, w_uq, w_ukv, sinks, w_mem_kv, w_out):
    B, S, _ = h.shape
    (c_q, c_kv, k_rope, g_mla, q_swa, k_swa, v_swa, g_swa,
     q_mem, g_mem) = split_cols(h @ w_in, EVEN_SPLITS)
    o_mla = mla_attention(c_q, c_kv, k_rope, pos, q_norm, kv_norm, w_uq, w_ukv) * jax.nn.silu(g_mla)
    o_swa = banded_attention(q_swa.reshape(B, S, SWA_HEADS, SWA_HEAD_DIM),
                             k_swa.reshape(B, S, SWA_KV_HEADS, SWA_HEAD_DIM),
                             v_swa.reshape(B, S, SWA_KV_HEADS, SWA_HEAD_DIM),
                             pos, table, SWA_WINDOW, sinks) * jax.nn.silu(g_swa)
    o_mem = memory_attention(q_mem, mem_n, w_mem_kv) * jax.nn.silu(g_mem)
    return jnp.concatenate([o_mla, o_swa, o_mem], axis=-1) @ w_out


def odd_layer(h, mem_n, pos, table, w_in, cmp_pos_k, cmp_pos_v, w1k, w2k, w1v, w2v, w_mem_kv, w_out):
    B, S, _ = h.shape
    (q, k_cmp, v_cmp, k_slc, v_slc, k_win, v_win, g_logit, g_nsa,
     q_mem, g_mem) = split_cols(h @ w_in, ODD_SPLITS)
    qh = q.reshape(B, S, NSA_HEADS, NSA_HEAD_DIM)
    kvh = lambda t: t.reshape(B, S, NSA_KV_HEADS, NSA_HEAD_DIM)
    o_cmp, o_slc = nsa_compressed_selected(qh, kvh(k_cmp), kvh(v_cmp), kvh(k_slc), kvh(v_slc), pos, table,
                                           cmp_pos_k, cmp_pos_v, w1k, w2k, w1v, w2v)
    o_win = banded_attention(qh, kvh(k_win), kvh(v_win), pos, table, NSA_WINDOW)
    g = jax.nn.sigmoid(g_logit.astype(jnp.float32)).reshape(B, S, NSA_HEADS, 3, 1).astype(h.dtype)
    per_head = lambda o: o.reshape(B, S, NSA_HEADS, NSA_HEAD_DIM)
    o_nsa = (g[..., 0, :] * per_head(o_cmp) + g[..., 1, :] * per_head(o_slc)
             + g[..., 2, :] * per_head(o_win)).reshape(B, S, NSA_W) * jax.nn.silu(g_nsa)
    o_mem = memory_attention(q_mem, mem_n, w_mem_kv) * jax.nn.silu(g_mem)
    return jnp.concatenate([o_nsa, o_mem], axis=-1) @ w_out


def setup_inputs(seed: int = 0) -> dict:
    key = jax.random.key(seed)
    ks = jax.random.split(key, 24)
    f32 = jnp.float32

    def nrm(k, shape, scale):
        return jax.random.normal(k, shape, f32) * scale

    def gain(k, shape):
        return 1.0 + 0.1 * jax.random.normal(k, shape, f32)

    offset = jax.random.randint(ks[2], (), 0, 1024, dtype=jnp.int32)
    d = NSA_HEAD_DIM
    return {
        "x": nrm(ks[0], (BATCH, SEQ, D_MODEL), 1.0),
        "mem": nrm(ks[1], (BATCH, MEM_LEN, D_MODEL), 1.0),
        "positions": offset + jnp.arange(SEQ, dtype=jnp.int32),
        "rel_bias_table": nrm(ks[3], (NUM_BUCKETS, BIAS_HEADS), 0.5),
        "norm_pre": gain(ks[4], (DEPTH, D_MODEL)),
        "norm_post": gain(ks[5], (DEPTH, D_MODEL)),
        "mem_norm": gain(ks[6], (DEPTH, D_MODEL)),
        "w_mem_kv": nrm(ks[7], (DEPTH, D_MODEL, 2 * MEM_W), D_MODEL ** -0.5),
        "w_in_even": nrm(ks[8], (N_EVEN, D_MODEL, EVEN_IN), D_MODEL ** -0.5),
        "mla_q_norm": gain(ks[9], (N_EVEN, MLA_Q_RANK)),
        "mla_kv_norm": gain(ks[10], (N_EVEN, MLA_KV_RANK)),
        "mla_w_uq": nrm(ks[11], (N_EVEN, MLA_Q_RANK, MLA_HEADS * (MLA_NOPE + MLA_ROPE)), MLA_Q_RANK ** -0.5),
        "mla_w_ukv": nrm(ks[12], (N_EVEN, MLA_KV_RANK, MLA_HEADS * (MLA_NOPE + MLA_V)), MLA_KV_RANK ** -0.5),
        "swa_sinks": nrm(ks[13], (N_EVEN, SWA_HEADS), 1.0),
        "w_out_even": nrm(ks[14], (N_EVEN, EVEN_OUT, D_MODEL), EVEN_OUT ** -0.5),
        "w_in_odd": nrm(ks[15], (N_ODD, D_MODEL, ODD_IN), D_MODEL ** -0.5),
        "cmp_pos_k": nrm(ks[16], (N_ODD, CMP_LEN, d), 0.1),
        "cmp_pos_v": nrm(ks[17], (N_ODD, CMP_LEN, d), 0.1),
        "cmp_w1_k": nrm(ks[18], (N_ODD, CMP_LEN * d, CMP_HIDDEN), (CMP_LEN * d) ** -0.5),
        "cmp_w2_k": nrm(ks[19], (N_ODD, CMP_HIDDEN, d), CMP_HIDDEN ** -0.5),
        "cmp_w1_v": nrm(ks[20], (N_ODD, CMP_LEN * d, CMP_HIDDEN), (CMP_LEN * d) ** -0.5),
        "cmp_w2_v": nrm(ks[21], (N_ODD, CMP_HIDDEN, d), CMP_HIDDEN ** -0.5),
        "w_out_odd": nrm(ks[22], (N_ODD, ODD_OUT, D_MODEL), ODD_OUT ** -0.5),
    }


def reference(x, mem, positions, rel_bias_table, norm_pre, norm_post, mem_norm, w_mem_kv,
              w_in_even, mla_q_norm, mla_kv_norm, mla_w_uq, mla_w_ukv, swa_sinks, w_out_even,
              w_in_odd, cmp_pos_k, cmp_pos_v, cmp_w1_k, cmp_w2_k, cmp_w1_v, cmp_w2_v, w_out_odd):
    for l in range(DEPTH):
        h = rms_norm(x, norm_pre[l])
        mem_n = rms_norm(mem, mem_norm[l])
        if l % 2 == 0:
            e = l // 2
            y = even_layer(h, mem_n, positions, rel_bias_table, w_in_even[e], mla_q_norm[e],
                           mla_kv_norm[e], mla_w_uq[e], mla_w_ukv[e], swa_sinks[e], w_mem_kv[l],
                           w_out_even[e])
        else:
            o = l // 2
            y = odd_layer(h, mem_n, positions, rel_bias_table, w_in_odd[o], cmp_pos_k[o], cmp_pos_v[o],
                          cmp_w1_k[o], cmp_w2_k[o], cmp_w1_v[o], cmp_w2_v[o], w_mem_kv[l], w_out_odd[o])
        x = x + rms_norm(y, norm_post[l])
    return x
```

```python
import functools
import math

import numpy as np
import jax
import jax.numpy as jnp
from jax import lax
from jax.experimental import pallas as pl
from jax.experimental.pallas import tpu as pltpu

F32 = jnp.float32
BF16 = jnp.bfloat16

D_MODEL = 1024
NEG_INF = -1e30
EPS = 1e-6
SEL_FORCE = 1e9
LOWEST = -3.0e38

NUM_BUCKETS = 32
REL_MAX_DIST = 128
N_SLOTS = 8

MLA_HEADS = 8
MLA_Q_RANK = 256
MLA_KV_RANK = 128
MLA_NOPE = 64
MLA_ROPE = 32
MLA_V = 64
ROPE_BASE = 10000.0
ROPE_HALF = MLA_ROPE // 2

HEAD_DIM = 64
N_HEADS = 8
N_KV = 2
SWA_WINDOW = 128
NSA_WINDOW = 512
CMP_LEN = 32
CMP_STRIDE = 16
CMP_HIDDEN = 256
SEL_LEN = 64
SEL_TOP_N = 16
MEM_HEADS = 4
MEM_HEAD_DIM = 128
MEM_W = MEM_HEADS * MEM_HEAD_DIM

LANES = 128
BAND_TQ = 128
MLA_TQ = 512
SLC_TQ = 256
SLC_TK = 512
PROJ_TM = 512
VMEM_LIMIT = 56 * 1024 * 1024


def _t5_thresholds():
    max_exact = NUM_BUCKETS // 2
    d = np.arange(0, 2 * REL_MAX_DIST)
    x = (np.log(np.maximum(d, 1).astype(np.float64) / max_exact)
         / math.log(REL_MAX_DIST / max_exact) * (NUM_BUCKETS - max_exact))
    frac = np.abs(x - np.round(x))[max_exact + 1:REL_MAX_DIST]
    assert frac.min() > 1e-3
    bucket = np.where(d < max_exact, d, np.minimum(max_exact + np.trunc(x).astype(np.int64), NUM_BUCKETS - 1))
    assert (np.diff(bucket) >= 0).all() and bucket[-1] == NUM_BUCKETS - 1
    return [int(np.argmax(bucket >= b)) for b in range(NUM_BUCKETS)]


T5_THR = _t5_thresholds()
FAR_DIST = T5_THR[NUM_BUCKETS - 1]

HEAD_ORDER = [h for p in range(4) for h in (p, 4 + p)]


def _cparams(sem):
    return pltpu.CompilerParams(dimension_semantics=sem, vmem_limit_bytes=VMEM_LIMIT)


def _dot(a, b):
    return jnp.dot(a, b, preferred_element_type=F32)


def _dot_nt(a, b):
    return lax.dot_general(a, b, (((1,), (1,)), ((), ())), preferred_element_type=F32)


def _rms(x, g):
    return x * lax.rsqrt(jnp.mean(x * x, axis=-1, keepdims=True) + EPS) * g


def _silu(x):
    return x * jax.nn.sigmoid(x)


def _bias_from_dist(dist, tab_ref, slot):
    cur = jnp.full(dist.shape, tab_ref[(NUM_BUCKETS - 1) * N_SLOTS + slot], F32)
    for b in range(NUM_BUCKETS - 2, -1, -1):
        cur = jnp.where(dist < T5_THR[b + 1], tab_ref[b * N_SLOTS + slot], cur)
    return cur


def _bias_kernel(tab_ref, pq_ref, pk_ref, o_ref):
    dist = pq_ref[0] - pk_ref[0]
    for s in range(N_SLOTS):
        o_ref[0, s] = _bias_from_dist(dist, tab_ref, s)


def _bias_tiles(tab_flat, pos_q, pos_k):
    nq, tq, _ = pos_q.shape
    kl = pos_k.shape[-1]
    return pl.pallas_call(
        _bias_kernel,
        grid=(nq,),
        in_specs=[pl.BlockSpec(memory_space=pltpu.SMEM),
                  pl.BlockSpec((1, tq, 1), lambda i: (i, 0, 0)),
                  pl.BlockSpec((1, 1, kl), lambda i: (i, 0, 0))],
        out_specs=pl.BlockSpec((1, N_SLOTS, tq, kl), lambda i: (i, 0, 0, 0)),
        out_shape=jax.ShapeDtypeStruct((nq, N_SLOTS, tq, kl), F32),
        compiler_params=_cparams(("arbitrary",)),
        name="bias_tiles",
    )(tab_flat, pos_q, pos_k)


def _memkv_kernel(mem_ref, g_ref, w_ref, k_ref, v_ref):
    mn = _rms(mem_ref[0], g_ref[0]).astype(BF16)
    kv = _dot(mn, w_ref[0])
    k_ref[0, 0] = kv[:, :MEM_W].astype(BF16)
    v_ref[0, 0] = kv[:, MEM_W:].astype(BF16)


def _mem_kv(mem, mem_norm, w_mem_kv):
    B, M, D = mem.shape
    L = mem_norm.shape[0]
    out = jax.ShapeDtypeStruct((L, B, M, MEM_W), BF16)
    return pl.pallas_call(
        _memkv_kernel,
        grid=(L, B),
        in_specs=[pl.BlockSpec((1, M, D), lambda l, b: (b, 0, 0)),
                  pl.BlockSpec((1, 1, D), lambda l, b: (l, 0, 0)),
                  pl.BlockSpec((1, D, 2 * MEM_W), lambda l, b: (l, 0, 0))],
        out_specs=[pl.BlockSpec((1, 1, M, MEM_W), lambda l, b: (l, b, 0, 0))] * 2,
        out_shape=[out, out],
        compiler_params=_cparams(("arbitrary", "arbitrary")),
        name="mem_kv",
    )(mem, mem_norm.reshape(L, 1, D), w_mem_kv.astype(BF16))


E_CQ, E_CKV, E_KR, E_KRS, E_GMLA, E_QSWA, E_KSWA, E_VSWA, E_GSWA, E_QMEM, E_GMEM, E_END = (
    0, 256, 384, 512, 640, 1152, 1664, 1792, 1920, 2432, 2944, 3456)


def _in_even_kernel(x_ref, gpre_ref, w_ref, qn_ref, kvn_ref, wq_ref, wqs_ref, wk_ref, wv_ref, c_ref, s_ref,
                    qmla_ref, kmla_ref, vmla_ref, gmla_ref, qswa_ref, kswa_ref, vswa_ref, gswa_ref,
                    qmem_ref, gmem_ref):
    h = _rms(x_ref[...], gpre_ref[...]).astype(BF16)

    def proj(lo, hi):
        return _dot(h, w_ref[:, lo:hi])

    cos = c_ref[...]
    sin = s_ref[...]
    cos8 = jnp.concatenate([cos] * MLA_HEADS, axis=1)
    sin8 = jnp.concatenate([sin] * MLA_HEADS, axis=1)

    cq = _rms(proj(E_CQ, E_CKV), qn_ref[...]).astype(BF16)
    q = _dot(cq, wq_ref[...]) * cos8 + _dot(cq, wqs_ref[...]) * sin8
    qmla_ref[...] = (q * (MLA_NOPE + MLA_ROPE) ** -0.5).astype(BF16)

    ckv = _rms(proj(E_CKV, E_KR), kvn_ref[...]).astype(BF16)
    kpe = proj(E_KR, E_KRS) * cos + proj(E_KRS, E_GMLA) * sin
    kmla_ref[...] = (_dot(ckv, wk_ref[...]) + jnp.concatenate([kpe] * MLA_HEADS, axis=1)).astype(BF16)
    vmla_ref[...] = _dot(ckv, wv_ref[...]).astype(BF16)

    gmla_ref[...] = proj(E_GMLA, E_QSWA).astype(BF16)
    qswa_ref[...] = proj(E_QSWA, E_KSWA).astype(BF16)
    kswa_ref[...] = proj(E_KSWA, E_VSWA).astype(BF16)
    vswa_ref[...] = proj(E_VSWA, E_GSWA).astype(BF16)
    gswa_ref[...] = proj(E_GSWA, E_QMEM).astype(BF16)
    qmem_ref[...] = (proj(E_QMEM, E_GMEM) * MEM_HEAD_DIM ** -0.5).astype(BF16)
    gmem_ref[...] = proj(E_GMEM, E_END).astype(BF16)


def _perm_heads(w, axis):
    idx = np.concatenate([np.arange(h * HEAD_DIM, (h + 1) * HEAD_DIM) for h in HEAD_ORDER])
    return jnp.take(w, idx, axis=axis)


def _even_weights(w_in, w_uq, w_ukv):
    D = w_in.shape[0]
    z = lambda n: jnp.zeros((D, n), F32)
    c = lambda lo, hi: w_in[:, lo:hi]
    w = jnp.concatenate([
        c(0, 256), c(256, 384),
        z(MLA_NOPE), c(384, 416), z(32),
        z(MLA_NOPE), c(400, 416), c(384, 400), z(32),
        c(416, 928),
        _perm_heads(c(928, 1440), 1) * HEAD_DIM ** -0.5, c(1440, 1568), c(1568, 1696),
        _perm_heads(c(1696, 2208), 1),
        c(2208, 2720), c(2720, 3232)], axis=1).astype(BF16)
    hq = MLA_NOPE + MLA_ROPE
    zq = lambda n: jnp.zeros((MLA_Q_RANK, n), F32)
    wq = jnp.concatenate([jnp.concatenate([w_uq[:, h * hq:(h + 1) * hq], zq(32)], axis=1)
                          for h in range(MLA_HEADS)], axis=1).astype(BF16)
    wqs = jnp.concatenate([jnp.concatenate([zq(MLA_NOPE),
                                            w_uq[:, h * hq + MLA_NOPE + ROPE_HALF:(h + 1) * hq],
                                            w_uq[:, h * hq + MLA_NOPE:h * hq + MLA_NOPE + ROPE_HALF],
                                            zq(32)], axis=1)
                           for h in range(MLA_HEADS)], axis=1).astype(BF16)
    hk = MLA_NOPE + MLA_V
    zk = jnp.zeros((MLA_KV_RANK, LANES - MLA_NOPE), F32)
    wk = jnp.concatenate([jnp.concatenate([w_ukv[:, h * hk:h * hk + MLA_NOPE], zk], axis=1)
                          for h in range(MLA_HEADS)], axis=1).astype(BF16)
    wv = jnp.concatenate([w_ukv[:, h * hk + MLA_NOPE:(h + 1) * hk] for h in range(MLA_HEADS)],
                         axis=1).astype(BF16)
    return w, wq, wqs, wk, wv


def _rope_tables(pos):
    inv = ROPE_BASE ** (-jnp.arange(ROPE_HALF, dtype=F32) / ROPE_HALF)
    ang = pos.astype(F32)[:, None] * inv[None, :]
    cos, sin = jnp.cos(ang), jnp.sin(ang)
    S = pos.shape[0]
    ones, z32 = jnp.ones((S, MLA_NOPE), F32), jnp.zeros((S, 32), F32)
    ctab = jnp.concatenate([ones, cos, cos, z32], axis=1)
    stab = jnp.concatenate([jnp.zeros((S, MLA_NOPE), F32), -sin, sin, z32], axis=1)
    return ctab, stab


def _full(shape):
    return pl.BlockSpec(shape, lambda i: (0,) * len(shape))


def _in_even(x2, gpre, w, qn, kvn, wq, wqs, wk, wv, ctab, stab):
    T, D = x2.shape
    S = ctab.shape[0]
    tm = PROJ_TM
    ns = S // tm
    row = lambda n: pl.BlockSpec((tm, n), lambda i: (i, 0))
    widths = [1024, 1024, 512, 512, 512, 128, 128, 512, 512, 512]
    return pl.pallas_call(
        _in_even_kernel,
        grid=(T // tm,),
        in_specs=[row(D), _full((1, D)), _full(w.shape), _full((1, MLA_Q_RANK)), _full((1, MLA_KV_RANK)),
                  _full(wq.shape), _full(wqs.shape), _full(wk.shape), _full(wv.shape),
                  pl.BlockSpec((tm, LANES), lambda i: (i % ns, 0)),
                  pl.BlockSpec((tm, LANES), lambda i: (i % ns, 0))],
        out_specs=[row(n) for n in widths],
        out_shape=[jax.ShapeDtypeStruct((T, n), BF16) for n in widths],
        compiler_params=_cparams(("arbitrary",)),
        name="in_proj_even",
    )(x2, gpre, w, qn, kvn, wq, wqs, wk, wv, ctab, stab)


def _mla_kernel(q_ref, k_ref, v_ref, o_ref, m_scr, l_scr, acc_scr, *, tq):
    i = pl.program_id(2)
    lane = lax.broadcasted_iota(jnp.int32, (tq, LANES), 1)
    m_scr[...] = jnp.full(m_scr.shape, NEG_INF, F32)
    l_scr[...] = jnp.zeros(l_scr.shape, F32)
    acc_scr[...] = jnp.zeros(acc_scr.shape, F32)

    def tile(j, diag):
        start = pl.multiple_of(j * tq, tq)
        v = v_ref[0, pl.ds(start, tq), :]
        pvs, alphas = [], []
        for t in range(2):
            q = q_ref[0, :, t * LANES:(t + 1) * LANES]
            k = k_ref[0, pl.ds(start, tq), t * LANES:(t + 1) * LANES]
            s = _dot_nt(q, k)
            if diag:
                r = lax.broadcasted_iota(jnp.int32, (tq, tq), 0)
                c = lax.broadcasted_iota(jnp.int32, (tq, tq), 1)
                s = jnp.where(r >= c, s, NEG_INF)
            m_old = m_scr[t]
            m_new = jnp.maximum(m_old, jnp.max(s, axis=-1, keepdims=True))
            alpha = jnp.exp(m_old - m_new)
            p = jnp.exp(s - m_new)
            l_scr[t] = alpha * l_scr[t] + jnp.sum(p, axis=-1, keepdims=True)
            m_scr[t] = m_new
            pvs.append(_dot(p.astype(BF16), v))
            alphas.append(alpha)
        first = lane < MLA_V
        acc_scr[...] = (acc_scr[...] * jnp.where(first, alphas[0], alphas[1])
                        + jnp.where(first, pvs[0], pvs[1]))

    def body(j, carry):
        tile(j, False)
        return carry

    lax.fori_loop(0, i, body, 0)
    tile(i, True)
    o_ref[0] = (acc_scr[...] / jnp.where(lane < MLA_V, l_scr[0], l_scr[1])).astype(BF16)


def _mla_attention(q, k, v):
    B, S, _ = q.shape
    tq = min(MLA_TQ, S)
    n_pairs = MLA_HEADS // 2
    return pl.pallas_call(
        functools.partial(_mla_kernel, tq=tq),
        grid=(B, n_pairs, S // tq),
        in_specs=[pl.BlockSpec((1, tq, 2 * LANES), lambda b, p, i: (b, i, p)),
                  pl.BlockSpec((1, S, 2 * LANES), lambda b, p, i: (b, 0, p)),
                  pl.BlockSpec((1, S, LANES), lambda b, p, i: (b, 0, p))],
        out_specs=pl.BlockSpec((1, tq, LANES), lambda b, p, i: (b, i, p)),
        out_shape=jax.ShapeDtypeStruct((B, S, MLA_HEADS * MLA_V), BF16),
        scratch_shapes=[pltpu.VMEM((2, tq, 1), F32), pltpu.VMEM((2, tq, 1), F32),
                        pltpu.VMEM((tq, LANES), F32)],
        compiler_params=_cparams(("arbitrary", "arbitrary", "arbitrary")),
        name="mla_flash",
    )(q, k, v)


def _local_kernel(*refs, nkb, window, use_sink, tq):
    refs = list(refs)
    sink_ref = refs.pop(0) if use_sink else None
    q_ref = refs.pop(0)
    k_refs = [refs.pop(0) for _ in range(nkb)]
    v_refs = [refs.pop(0) for _ in range(nkb)]
    bias_ref, qm_ref, km_ref, vm_ref, o_ref, om_ref = refs

    i = pl.program_id(0)
    pad = (nkb - 1) * tq
    kl = nkb * tq
    k = jnp.concatenate([r[0] for r in k_refs], axis=0)
    v = jnp.concatenate([r[0] for r in v_refs], axis=0)
    r = lax.broadcasted_iota(jnp.int32, (tq, kl), 0)
    c = lax.broadcasted_iota(jnp.int32, (tq, kl), 1)
    delta = r + pad - c
    mask = ((c + (i * tq - pad)) >= 0) & (delta >= 0) & (delta < window)
    first = lax.broadcasted_iota(jnp.int32, (tq, LANES), 1) < HEAD_DIM

    blocks = []
    for p in range(4):
        qp = q_ref[0, :, p * LANES:(p + 1) * LANES]
        halves = []
        for t in range(2):
            qs = jnp.where(first if t == 0 else ~first, qp, jnp.zeros_like(qp))
            s = _dot_nt(qs, k) + bias_ref[0, 2 * p + t]
            s = jnp.where(mask, s, NEG_INF)
            m = jnp.max(s, axis=-1, keepdims=True)
            if use_sink:
                sink = sink_ref[2 * p + t]
                m = jnp.maximum(m, sink)
            pr = jnp.exp(s - m)
            den = jnp.sum(pr, axis=-1, keepdims=True)
            if use_sink:
                den = den + jnp.exp(sink - m)
            halves.append(_dot(pr.astype(BF16), v) / den)
        blocks.append(jnp.where(first, halves[0], halves[1]))
    o_ref[0] = jnp.concatenate(blocks, axis=1).astype(BF16)

    mem_blocks = []
    for h in range(MEM_HEADS):
        sl = slice(h * MEM_HEAD_DIM, (h + 1) * MEM_HEAD_DIM)
        s = _dot_nt(qm_ref[0, :, sl], km_ref[0, :, sl])
        pr = jnp.exp(s - jnp.max(s, axis=-1, keepdims=True))
        mem_blocks.append(_dot(pr.astype(BF16), vm_ref[0, :, sl]) / jnp.sum(pr, axis=-1, keepdims=True))
    om_ref[0] = jnp.concatenate(mem_blocks, axis=1).astype(BF16)


def _local_attention(q, k, v, bias, qm, km, vm, window, sinks=None):
    B, S, W = q.shape
    tq = BAND_TQ
    nkb = -(-(window - 1) // tq) + 1
    M = km.shape[1]
    use_sink = sinks is not None

    def kv_spec(back):
        return pl.BlockSpec((1, tq, LANES), lambda i, b: (b, jnp.maximum(i - back, 0), 0))

    kv_specs = [kv_spec(nkb - 1 - n) for n in range(nkb)]
    in_specs = ([pl.BlockSpec(memory_space=pltpu.SMEM)] if use_sink else []) + (
        [pl.BlockSpec((1, tq, W), lambda i, b: (b, i, 0))] + kv_specs + kv_specs
        + [pl.BlockSpec((1, N_SLOTS, tq, nkb * tq), lambda i, b: (i, 0, 0, 0)),
           pl.BlockSpec((1, tq, MEM_W), lambda i, b: (b, i, 0)),
           pl.BlockSpec((1, M, MEM_W), lambda i, b: (b, 0, 0)),
           pl.BlockSpec((1, M, MEM_W), lambda i, b: (b, 0, 0))])
    args = ([sinks] if use_sink else []) + [q] + [k] * nkb + [v] * nkb + [bias, qm, km, vm]
    return pl.pallas_call(
        functools.partial(_local_kernel, nkb=nkb, window=window, use_sink=use_sink, tq=tq),
        grid=(S // tq, B),
        in_specs=in_specs,
        out_specs=[pl.BlockSpec((1, tq, W), lambda i, b: (b, i, 0)),
                   pl.BlockSpec((1, tq, MEM_W), lambda i, b: (b, i, 0))],
        out_shape=[jax.ShapeDtypeStruct((B, S, W), BF16), jax.ShapeDtypeStruct((B, S, MEM_W), BF16)],
        compiler_params=_cparams(("arbitrary", "arbitrary")),
        name="local_attention_w%d" % window,
    )(*args)


def _gate(o_ref, g_ref):
    return (o_ref[...].astype(F32) * _silu(g_ref[...].astype(F32))).astype(BF16)


def _out_even_kernel(x_ref, o1, g1, o2, g2, o3, g3, w_ref, gpost_ref, y_ref):
    W = o1.shape[1]
    y = (_dot(_gate(o1, g1), w_ref[0:W, :]) + _dot(_gate(o2, g2), w_ref[W:2 * W, :])
         + _dot(_gate(o3, g3), w_ref[2 * W:3 * W, :]))
    y_ref[...] = x_ref[...] + _rms(y, gpost_ref[...])


def _out_even(x2, o1, g1, o2, g2, o3, g3, w, gpost):
    T, D = x2.shape
    tm = PROJ_TM
    row = lambda n: pl.BlockSpec((tm, n), lambda i: (i, 0))
    return pl.pallas_call(
        _out_even_kernel,
        grid=(T // tm,),
        in_specs=[row(D)] + [row(512)] * 6 + [_full(w.shape), _full((1, D))],
        out_specs=row(D),
        out_shape=jax.ShapeDtypeStruct((T, D), F32),
        compiler_params=_cparams(("arbitrary",)),
        name="out_proj_even",
    )(x2, o1, g1, o2, g2, o3, g3, w, gpost)


def _out_odd_kernel(x_ref, oc, os_, ow, gl_ref, gn, om, gm, ex_ref, w_ref, gpost_ref, y_ref):
    W = oc.shape[1]
    sig = jax.nn.sigmoid(gl_ref[...])
    hi = sig.astype(BF16)
    lo = (sig - hi.astype(F32)).astype(BF16)
    mix = jnp.zeros(oc.shape, F32)
    for br, o_ref in enumerate((oc, os_, ow)):
        g = _dot(hi, ex_ref[br]) + _dot(lo, ex_ref[br])
        mix = mix + g * o_ref[...].astype(F32)
    nsa = (mix * _silu(gn[...].astype(F32))).astype(BF16)
    y = _dot(nsa, w_ref[0:W, :]) + _dot(_gate(om, gm), w_ref[W:2 * W, :])
    y_ref[...] = x_ref[...] + _rms(y, gpost_ref[...])


def _out_odd(x2, oc, os_, ow, gl, gn, om, gm, ex, w, gpost):
    T, D = x2.shape
    tm = PROJ_TM
    row = lambda n: pl.BlockSpec((tm, n), lambda i: (i, 0))
    return pl.pallas_call(
        _out_odd_kernel,
        grid=(T // tm,),
        in_specs=[row(D), row(512), row(512), row(512), row(LANES), row(512), row(512), row(512),
                  _full(ex.shape), _full(w.shape), _full((1, D))],
        out_specs=row(D),
        out_shape=jax.ShapeDtypeStruct((T, D), F32),
        compiler_params=_cparams(("arbitrary",)),
        name="out_proj_odd",
    )(x2, oc, os_, ow, gl, gn, om, gm, ex, w, gpost)


O_Q, O_KC, O_VC, O_KS, O_VS, O_KW, O_VW, O_GL, O_GN, O_QM, O_GM, O_END = (
    0, 512, 640, 768, 896, 1024, 1152, 1280, 1408, 1920, 2432, 2944)


def _in_odd_kernel(x_ref, gpre_ref, w_ref, q_ref, kc_ref, vc_ref, ks_ref, vs_ref, kw_ref, vw_ref,
                   gl_ref, gn_ref, qm_ref, gm_ref):
    h = _rms(x_ref[...], gpre_ref[...]).astype(BF16)

    def proj(lo, hi):
        return _dot(h, w_ref[:, lo:hi])

    q_ref[...] = proj(O_Q, O_KC).astype(BF16)
    kc_ref[...] = proj(O_KC, O_VC)
    vc_ref[...] = proj(O_VC, O_KS)
    ks_ref[...] = proj(O_KS, O_VS).astype(BF16)
    vs_ref[...] = proj(O_VS, O_KW).astype(BF16)
    kw_ref[...] = proj(O_KW, O_VW).astype(BF16)
    vw_ref[...] = proj(O_VW, O_GL).astype(BF16)
    gl_ref[...] = proj(O_GL, O_GN)
    gn_ref[...] = proj(O_GN, O_QM).astype(BF16)
    qm_ref[...] = (proj(O_QM, O_GM) * MEM_HEAD_DIM ** -0.5).astype(BF16)
    gm_ref[...] = proj(O_GM, O_END).astype(BF16)


def _odd_weights(w_in):
    D = w_in.shape[0]
    c = lambda lo, hi: w_in[:, lo:hi]
    return jnp.concatenate([
        _perm_heads(c(0, 512), 1) * HEAD_DIM ** -0.5,
        c(512, 1280),
        c(1280, 1304), jnp.zeros((D, LANES - 3 * N_HEADS), F32),
        _perm_heads(c(1304, 1816), 1),
        c(1816, 2328), c(2328, 2840)], axis=1).astype(BF16)


def _in_odd(x2, gpre, w):
    T, D = x2.shape
    tm = PROJ_TM
    row = lambda n: pl.BlockSpec((tm, n), lambda i: (i, 0))
    widths = [512, 128, 128, 128, 128, 128, 128, 128, 512, 512, 512]
    dtypes = [BF16, F32, F32, BF16, BF16, BF16, BF16, F32, BF16, BF16, BF16]
    return pl.pallas_call(
        _in_odd_kernel,
        grid=(T // tm,),
        in_specs=[row(D), _full((1, D)), _full(w.shape)],
        out_specs=[row(n) for n in widths],
        out_shape=[jax.ShapeDtypeStruct((T, n), dt) for n, dt in zip(widths, dtypes)],
        compiler_params=_cparams(("arbitrary",)),
        name="in_proj_odd",
    )(x2, gpre, w)


def _compress_kernel(rk_ref, rv_ref, pek_ref, pev_ref, w1k_ref, w1v_ref, w2k_ref, w2v_ref, kc_ref, vc_ref, *, nc):
    def one(r_ref, pe_ref, w1_ref, w2_ref, o_ref):
        r = r_ref[0]
        n = r.shape[0]
        za = _dot((r + pe_ref[0:1, :]).astype(BF16), w1_ref[0])
        zb = _dot((r + pe_ref[1:2, :]).astype(BF16), w1_ref[1])
        z = za + pltpu.roll(zb, n - 1, 0)
        out = _dot(_silu(z).astype(BF16), w2_ref[...])
        row = lax.broadcasted_iota(jnp.int32, out.shape, 0)
        o_ref[0] = jnp.where(row < nc, out, 0.0).astype(BF16)

    one(rk_ref, pek_ref, w1k_ref, w2k_ref, kc_ref)
    one(rv_ref, pev_ref, w1v_ref, w2v_ref, vc_ref)


def _compress_weights(pe, w1, w2):
    eye = jnp.eye(N_KV, dtype=F32)
    d = HEAD_DIM
    w1h = w1.reshape(2, CMP_STRIDE, d, CMP_HIDDEN)
    w1x = jnp.einsum('ztdn,gh->ztgdhn', w1h, eye).reshape(2, CMP_STRIDE * N_KV * d, N_KV * CMP_HIDDEN)
    w2x = jnp.einsum('kn,gh->gkhn', w2, eye).reshape(N_KV * CMP_HIDDEN, N_KV * d)
    pex = jnp.broadcast_to(pe.reshape(2, CMP_STRIDE, 1, d), (2, CMP_STRIDE, N_KV, d)).reshape(2, -1)
    return pex, w1x.astype(BF16), w2x.astype(BF16)


def _compress(rk, rv, pek, pev, w1k, w1v, w2k, w2v):
    B, n, W = rk.shape
    spec3 = lambda a: pl.BlockSpec(a.shape, lambda b: (0,) * a.ndim)
    rspec = pl.BlockSpec((1, n, W), lambda b: (b, 0, 0))
    ospec = pl.BlockSpec((1, n, LANES), lambda b: (b, 0, 0))
    out = jax.ShapeDtypeStruct((B, n, LANES), BF16)
    return pl.pallas_call(
        functools.partial(_compress_kernel, nc=n - 1),
        grid=(B,),
        in_specs=[rspec, rspec, spec3(pek), spec3(pev), spec3(w1k), spec3(w1v), spec3(w2k), spec3(w2v)],
        out_specs=[ospec, ospec],
        out_shape=[out, out],
        compiler_params=_cparams(("arbitrary",)),
        name="compress_blocks",
    )(rk, rv, pek, pev, w1k, w1v, w2k, w2v)


def _cmp_kernel(q_ref, kc_ref, vc_ref, bias_ref, m_ref, o_ref, sel_ref, *, tq, ns, top_n):
    i = pl.program_id(0)
    n = kc_ref.shape[1]
    kc = kc_ref[0]
    vc = vc_ref[0]
    r = lax.broadcasted_iota(jnp.int32, (tq, n), 0)
    c = lax.broadcasted_iota(jnp.int32, (tq, n), 1)
    mask = (c * CMP_STRIDE + (CMP_LEN - 1)) <= (r + i * tq)
    lane = lax.broadcasted_iota(jnp.int32, (tq, LANES), 1)
    first = lane < HEAD_DIM

    psum = [jnp.zeros((tq, n), F32), jnp.zeros((tq, n), F32)]
    blocks = []
    for p in range(4):
        qp = q_ref[0, :, p * LANES:(p + 1) * LANES]
        halves = []
        for t in range(2):
            qs = jnp.where(first if t == 0 else ~first, qp, jnp.zeros_like(qp))
            s = _dot_nt(qs, kc) + bias_ref[0, 2 * p + t]
            s = jnp.where(mask, s, NEG_INF)
            m = jnp.max(s, axis=-1, keepdims=True)
            pr = jnp.where(mask, jnp.exp(s - m), 0.0)
            pr = pr / jnp.maximum(jnp.sum(pr, axis=-1, keepdims=True), 1e-30)
            psum[t] = psum[t] + pr
            halves.append(_dot(pr.astype(BF16), vc))
        blocks.append(jnp.where(first, halves[0], halves[1]))
    o_ref[0] = jnp.concatenate(blocks, axis=1).astype(BF16)

    q_idx = lax.broadcasted_iota(jnp.int32, (tq, LANES), 0) + i * tq
    cur = q_idx // SEL_LEN
    forced = (lane == 0) | (lane == cur) | (lane == cur - 1)
    causal = lane * SEL_LEN <= q_idx
    for t in range(2):
        hi = psum[t].astype(BF16)
        lo = (psum[t] - hi.astype(F32)).astype(BF16)
        imp = _dot(hi, m_ref[...]) + _dot(lo, m_ref[...])
        score = jnp.where(forced, SEL_FORCE, jnp.where(causal, imp, -SEL_FORCE))
        score = jnp.where(lane < ns, score, LOWEST)
        sel = jnp.zeros((tq, LANES), jnp.bool_)
        for _ in range(top_n):
            mx = jnp.max(score, axis=-1, keepdims=True)
            idx = jnp.min(jnp.where(score == mx, lane, LANES), axis=-1, keepdims=True)
            pick = lane == idx
            sel = sel | pick
            score = jnp.where(pick, LOWEST, score)
        sel_ref[0, t] = jnp.where(sel & causal, 1.0, 0.0).astype(BF16)


def _cmp_to_sel(n, ns):
    c_start = np.arange(n)[:, None] * CMP_STRIDE
    s_start = np.arange(LANES)[None, :] * SEL_LEN
    overlap = np.maximum(np.minimum(c_start + CMP_LEN, s_start + SEL_LEN) - np.maximum(c_start, s_start), 0)
    m = overlap.astype(np.float32) / CMP_LEN
    m[n - 1:, :] = 0.0
    m[:, ns:] = 0.0
    return jnp.asarray(m, BF16)


def _cmp_select(q, kc, vc, bias, m):
    B, S, W = q.shape
    n = kc.shape[1]
    tq = BAND_TQ
    ns = S // SEL_LEN
    return pl.pallas_call(
        functools.partial(_cmp_kernel, tq=tq, ns=ns, top_n=min(SEL_TOP_N, ns)),
        grid=(S // tq, B),
        in_specs=[pl.BlockSpec((1, tq, W), lambda i, b: (b, i, 0)),
                  pl.BlockSpec((1, n, LANES), lambda i, b: (b, 0, 0)),
                  pl.BlockSpec((1, n, LANES), lambda i, b: (b, 0, 0)),
                  pl.BlockSpec((1, N_SLOTS, tq, n), lambda i, b: (i, 0, 0, 0)),
                  pl.BlockSpec(m.shape, lambda i, b: (0, 0))],
        out_specs=[pl.BlockSpec((1, tq, W), lambda i, b: (b, i, 0)),
                   pl.BlockSpec((1, N_KV, tq, LANES), lambda i, b: (b, 0, i, 0))],
        out_shape=[jax.ShapeDtypeStruct((B, S, W), BF16),
                   jax.ShapeDtypeStruct((B, N_KV, S, LANES), BF16)],
        compiler_params=_cparams(("arbitrary", "arbitrary")),
        name="cmp_select",
    )(q, kc, vc, bias, m)


def _slc_kernel(qmin_ref, kmax_ref, tab_ref, q_ref, sel_ref, pq_ref, pk_ref, k_ref, v_ref, o_ref,
                m_scr, l_scr, acc_scr, *, tq, tk):
    i = pl.program_id(1)
    lane = lax.broadcasted_iota(jnp.int32, (tq, LANES), 1)
    first = lane < HEAD_DIM
    m_scr[...] = jnp.full(m_scr.shape, NEG_INF, F32)
    l_scr[...] = jnp.zeros(l_scr.shape, F32)
    acc_scr[...] = jnp.zeros(acc_scr.shape, F32)

    qs = []
    for t in range(2):
        parts = []
        for p in range(4):
            qp = q_ref[0, :, p * LANES:(p + 1) * LANES]
            parts.append(jnp.where(first if t == 0 else ~first, qp, jnp.zeros_like(qp)))
        qs.append(jnp.concatenate(parts, axis=0))

    blk_row = lax.broadcasted_iota(jnp.int32, (LANES, tk), 0)
    blk_col = lax.broadcasted_iota(jnp.int32, (LANES, tk), 1) // SEL_LEN
    last = (i * tq + tq - 1) // tk

    def tile(j, near, diag):
        start = pl.multiple_of(j * tk, tk)
        k = k_ref[0, pl.ds(start, tk), :]
        v = v_ref[0, pl.ds(start, tk), :]
        expand = jnp.where(blk_row == blk_col + j * (tk // SEL_LEN), 1.0, 0.0).astype(BF16)
        if near:
            dist = pq_ref[0] - pk_ref[j]
        if diag:
            q_idx = lax.broadcasted_iota(jnp.int32, (tq, tk), 0) + i * tq
            k_idx = lax.broadcasted_iota(jnp.int32, (tq, tk), 1) + j * tk
            tok_ok = k_idx <= q_idx
        for t in range(2):
            chosen = _dot(sel_ref[0, t], expand) > 0.5
            if diag:
                chosen = chosen & tok_ok
            mask_add = jnp.where(chosen, 0.0, NEG_INF)
            s = _dot_nt(qs[t], k)
            ps = []
            for p in range(4):
                rows = slice(p * tq, (p + 1) * tq)
                slot = 2 * p + t
                if near:
                    bias = _bias_from_dist(dist, tab_ref, slot)
                else:
                    bias = tab_ref[(NUM_BUCKETS - 1) * N_SLOTS + slot]
                sp = s[rows] + bias + mask_add
                m_old = m_scr[t, rows]
                m_new = jnp.maximum(m_old, jnp.max(sp, axis=-1, keepdims=True))
                alpha = jnp.exp(m_old - m_new)
                pr = jnp.exp(sp - m_new)
                l_scr[t, rows] = alpha * l_scr[t, rows] + jnp.sum(pr, axis=-1, keepdims=True)
                m_scr[t, rows] = m_new
                acc_scr[t, rows] = alpha * acc_scr[t, rows]
                ps.append(pr.astype(BF16))
            acc_scr[t] = acc_scr[t] + _dot(jnp.concatenate(ps, axis=0), v)

    def body(j, carry):
        far = (qmin_ref[i] - kmax_ref[j]) >= FAR_DIST

        @pl.when(far)
        def _():
            tile(j, False, False)

        @pl.when(jnp.logical_not(far))
        def _():
            tile(j, True, False)

        return carry

    lax.fori_loop(0, last, body, 0)
    tile(last, True, True)

    blocks = []
    for p in range(4):
        rows = slice(p * tq, (p + 1) * tq)
        blocks.append(jnp.where(first, acc_scr[0, rows] / l_scr[0, rows], acc_scr[1, rows] / l_scr[1, rows]))
    o_ref[0] = jnp.concatenate(blocks, axis=1).astype(BF16)


def _slc_attention(q, sel, k, v, pos, tab_flat):
    B, S, W = q.shape
    tq = min(SLC_TQ, S)
    tk = min(SLC_TK, S)
    nq, nk = S // tq, S // tk
    qmin = pos.reshape(nq, tq).min(axis=1)
    kmax = pos.reshape(nk, tk).max(axis=1)
    pq = pos.reshape(nq, tq, 1)
    pk = pos.reshape(nk, 1, tk)
    smem = pl.BlockSpec(memory_space=pltpu.SMEM)
    return pl.pallas_call(
        functools.partial(_slc_kernel, tq=tq, tk=tk),
        grid=(B, nq),
        in_specs=[smem, smem, smem,
                  pl.BlockSpec((1, tq, W), lambda b, i: (b, i, 0)),
                  pl.BlockSpec((1, N_KV, tq, LANES), lambda b, i: (b, 0, i, 0)),
                  pl.BlockSpec((1, tq, 1), lambda b, i: (i, 0, 0)),
                  pl.BlockSpec((nk, 1, tk), lambda b, i: (0, 0, 0)),
                  pl.BlockSpec((1, S, LANES), lambda b, i: (b, 0, 0)),
                  pl.BlockSpec((1, S, LANES), lambda b, i: (b, 0, 0))],
        out_specs=pl.BlockSpec((1, tq, W), lambda b, i: (b, i, 0)),
        out_shape=jax.ShapeDtypeStruct((B, S, W), BF16),
        scratch_shapes=[pltpu.VMEM((N_KV, 4 * tq, 1), F32), pltpu.VMEM((N_KV, 4 * tq, 1), F32),
                        pltpu.VMEM((N_KV, 4 * tq, LANES), F32)],
        compiler_params=_cparams(("arbitrary", "arbitrary")),
        name="slc_attention",
    )(qmin, kmax, tab_flat, q, sel, pq, pk, k, v)


def _band_positions(pos, tq, pad):
    S = pos.shape[0]
    nq = S // tq
    posp = jnp.pad(pos, (pad, 0))
    win = posp[(jnp.arange(nq) * tq)[:, None] + jnp.arange(pad + tq)[None, :]]
    return pos.reshape(nq, tq, 1), win.reshape(nq, 1, pad + tq)


def _gate_expand():
    ex = np.zeros((3, LANES, N_HEADS * HEAD_DIM), np.float32)
    for slot, h in enumerate(HEAD_ORDER):
        for br in range(3):
            ex[br, 3 * h + br, slot * HEAD_DIM:(slot + 1) * HEAD_DIM] = 1.0
    return jnp.asarray(ex, BF16)


def kernel(x, mem, positions, rel_bias_table, norm_pre, norm_post, mem_norm, w_mem_kv, w_in_even, mla_q_norm,
           mla_kv_norm, mla_w_uq, mla_w_ukv, swa_sinks, w_out_even, w_in_odd, cmp_pos_k, cmp_pos_v, cmp_w1_k,
           cmp_w2_k, cmp_w1_v, cmp_w2_v, w_out_odd):
    B, S, D = x.shape
    depth = norm_pre.shape[0]
    T = B * S
    pos = positions.astype(jnp.int32)
    n_cmp = S // CMP_STRIDE

    order = np.asarray(HEAD_ORDER)
    tab_flat = rel_bias_table.astype(F32)[:, order].reshape(-1)
    pq, pk_swa = _band_positions(pos, BAND_TQ, BAND_TQ)
    bias_swa = _bias_tiles(tab_flat, pq, pk_swa)
    _, pk_win = _band_positions(pos, BAND_TQ, (-(-(NSA_WINDOW - 1) // BAND_TQ)) * BAND_TQ)
    bias_win = _bias_tiles(tab_flat, pq, pk_win)
    c_end = jnp.minimum(jnp.arange(n_cmp) * CMP_STRIDE + CMP_LEN - 1, S - 1)
    pk_cmp = jnp.broadcast_to(pos[c_end][None, None, :], (S // BAND_TQ, 1, n_cmp))
    bias_cmp = _bias_tiles(tab_flat, pq, pk_cmp)

    ctab, stab = _rope_tables(pos)
    mem_k, mem_v = _mem_kv(mem, mem_norm, w_mem_kv)
    m_sel = _cmp_to_sel(n_cmp, S // SEL_LEN)
    gate_ex = _gate_expand()

    x2 = x.reshape(T, D)
    r3 = lambda a: a.reshape(B, S, a.shape[-1])
    for l in range(depth):
        gpre = norm_pre[l].reshape(1, D)
        gpost = norm_post[l].reshape(1, D)
        if l % 2 == 0:
            e = l // 2
            w, wq, wqs, wk, wv = _even_weights(w_in_even[e], mla_w_uq[e], mla_w_ukv[e])
            (qmla, kmla, vmla, gmla, qswa, kswa, vswa, gswa, qmem, gmem) = _in_even(
                x2, gpre, w, mla_q_norm[e].reshape(1, -1), mla_kv_norm[e].reshape(1, -1),
                wq, wqs, wk, wv, ctab, stab)
            o_mla = _mla_attention(r3(qmla), r3(kmla), r3(vmla))
            o_swa, o_mem = _local_attention(r3(qswa), r3(kswa), r3(vswa), bias_swa, r3(qmem),
                                            mem_k[l], mem_v[l], SWA_WINDOW, sinks=swa_sinks[e][order])
            w_o = w_out_even[e]
            w_o = jnp.concatenate([w_o[:512], _perm_heads(w_o[512:1024], 0), w_o[1024:]], axis=0).astype(BF16)
            x2 = _out_even(x2, o_mla.reshape(T, -1), gmla, o_swa.reshape(T, -1), gswa,
                           o_mem.reshape(T, -1), gmem, w_o, gpost)
        else:
            o = l // 2
            w = _odd_weights(w_in_odd[o])
            (q, kc_in, vc_in, ks, vs, kw, vw, gl, gn, qmem, gmem) = _in_odd(x2, gpre, w)
            pek, w1k, w2k = _compress_weights(cmp_pos_k[o], cmp_w1_k[o], cmp_w2_k[o])
            pev, w1v, w2v = _compress_weights(cmp_pos_v[o], cmp_w1_v[o], cmp_w2_v[o])
            chunks = lambda a: a.reshape(B, n_cmp, CMP_STRIDE * LANES)
            kc, vc = _compress(chunks(kc_in), chunks(vc_in), pek, pev, w1k, w1v, w2k, w2v)
            o_cmp, sel = _cmp_select(r3(q), kc, vc, bias_cmp, m_sel)
            o_slc = _slc_attention(r3(q), sel, r3(ks), r3(vs), pos, tab_flat)
            o_win, o_mem = _local_attention(r3(q), r3(kw), r3(vw), bias_win, r3(qmem),
                                            mem_k[l], mem_v[l], NSA_WINDOW)
            w_o = w_out_odd[o]
            w_o = jnp.concatenate([_perm_heads(w_o[:512], 0), w_o[512:]], axis=0).astype(BF16)
            x2 = _out_odd(x2, o_cmp.reshape(T, -1), o_slc.reshape(T, -1), o_win.reshape(T, -1), gl, gn,
                          o_mem.reshape(T, -1), gmem, gate_ex, w_o, gpost)
    return x2.reshape(B, S, D)
```

```python
import functools
import math

import numpy as np
import jax
import jax.numpy as jnp
from jax import lax
from jax.experimental import pallas as pl
from jax.experimental.pallas import tpu as pltpu

F32 = jnp.float32
BF16 = jnp.bfloat16

NEG_INF = -1e30
EPS = 1e-6
SEL_FORCE = 1e9
LOWEST = -3.0e38
LOG2E = math.log2(math.e)

NUM_BUCKETS = 32
REL_MAX_DIST = 128

MLA_HEADS = 8
MLA_Q_RANK = 256
MLA_KV_RANK = 128
MLA_NOPE = 64
MLA_ROPE = 32
MLA_V = 64
ROPE_BASE = 10000.0
ROPE_HALF = MLA_ROPE // 2

HEAD_DIM = 64
N_HEADS = 8
N_KV = 2
GROUP = N_HEADS // N_KV
SWA_WINDOW = 128
NSA_WINDOW = 512
CMP_LEN = 32
CMP_STRIDE = 16
CMP_HIDDEN = 256
SEL_LEN = 64
SEL_TOP_N = 16
MEM_HEADS = 4
MEM_HEAD_DIM = 128
MEM_W = MEM_HEADS * MEM_HEAD_DIM

LANES = 128
BF16_ROWS = 16
ACC_ROWS = HEAD_DIM + BF16_ROWS
BAND_TQ = 128
CMP_TQ = 256
MLA_TQ = 512
SLC_TQ = 256
SLC_TK = 512
PROJ_TM = 512
VMEM_LIMIT = 56 * 1024 * 1024


def _t5_far_distance():
    max_exact = NUM_BUCKETS // 2
    d = np.arange(0, 2 * REL_MAX_DIST)
    x = (np.log(np.maximum(d, 1).astype(np.float64) / max_exact)
         / math.log(REL_MAX_DIST / max_exact) * (NUM_BUCKETS - max_exact))
    assert np.abs(x - np.round(x))[max_exact + 1:REL_MAX_DIST].min() > 1e-3
    bucket = np.where(d < max_exact, d, np.minimum(max_exact + np.trunc(x).astype(np.int64), NUM_BUCKETS - 1))
    assert (np.diff(bucket) >= 0).all()
    return int(np.argmax(bucket >= NUM_BUCKETS - 1))


FAR_DIST = _t5_far_distance()

HEAD_ORDER = [h for p in range(GROUP) for h in (p, GROUP + p)]


def _cparams(sem):
    return pltpu.CompilerParams(dimension_semantics=sem, vmem_limit_bytes=VMEM_LIMIT)


def _dot(a, b):
    return jnp.dot(a, b, preferred_element_type=F32)


def _dot_nt(a, b):
    return lax.dot_general(a, b, (((1,), (1,)), ((), ())), preferred_element_type=F32)


def _rms(x, g):
    return x * lax.rsqrt(jnp.mean(x * x, axis=-1, keepdims=True) + EPS) * g


def _silu(x):
    return x * jax.nn.sigmoid(x)


def _t5_bucket(dist):
    max_exact = NUM_BUCKETS // 2
    d = jnp.maximum(dist, 0.0)
    scale = (NUM_BUCKETS - max_exact) / math.log2(REL_MAX_DIST / max_exact)
    large = max_exact + jnp.floor(jnp.log2(jnp.maximum(d, 1.0) * (1.0 / max_exact)) * scale)
    return jnp.where(d < max_exact, d, jnp.minimum(large, NUM_BUCKETS - 1.0)).astype(jnp.int32)


def _bias_lookup(bucket, tab_ref, head):
    rows = jnp.broadcast_to(tab_ref[head:head + 1, :], (bucket.shape[0], LANES))
    return jnp.concatenate(
        [jnp.take_along_axis(rows, bucket[:, c * LANES:(c + 1) * LANES], axis=1)
         for c in range(bucket.shape[1] // LANES)], axis=1)


def _split_groups(q_ref, tq):
    first = lax.broadcasted_iota(jnp.int32, (tq, LANES), 1) < HEAD_DIM
    parts = []
    for t in range(N_KV):
        for p in range(GROUP):
            qp = q_ref[0, :, p * LANES:(p + 1) * LANES]
            parts.append(jnp.where(first if t == 0 else ~first, qp, jnp.zeros_like(qp)))
    return jnp.concatenate(parts, axis=0)


def _merge_heads(out_t):
    blocks = [jnp.concatenate([out_t[p], out_t[GROUP + p]], axis=0).T for p in range(GROUP)]
    return jnp.concatenate(blocks, axis=1)


def _bias_kernel(tab_ref, pq_ref, pk_ref, o_ref, *, tq):
    bucket = _t5_bucket((pq_ref[0] - pk_ref[0]).astype(F32))
    for h in range(N_HEADS):
        o_ref[0, :, h * tq:(h + 1) * tq] = _bias_lookup(bucket, tab_ref, h)


def _bias_tiles(tab, pos_q, pos_k, per_tile_keys):
    nq, _, tq = pos_q.shape
    kl = pos_k.shape[1]
    return pl.pallas_call(
        functools.partial(_bias_kernel, tq=tq),
        grid=(nq,),
        in_specs=[pl.BlockSpec(tab.shape, lambda i: (0, 0)),
                  pl.BlockSpec((1, 1, tq), lambda i: (i, 0, 0)),
                  pl.BlockSpec((1, kl, 1), (lambda i: (i, 0, 0)) if per_tile_keys else (lambda i: (0, 0, 0)))],
        out_specs=pl.BlockSpec((1, kl, N_HEADS * tq), lambda i: (i, 0, 0)),
        out_shape=jax.ShapeDtypeStruct((nq, kl, N_HEADS * tq), F32),
        compiler_params=_cparams(("arbitrary",)),
        name="bias_tiles",
    )(tab, pos_q, pos_k)


def _memkv_kernel(mem_ref, g_ref, wk_ref, wvt_ref, k_ref, vt_ref):
    mn = _rms(mem_ref[0], g_ref[0]).astype(BF16)
    k_ref[0, 0] = _dot(mn, wk_ref[0]).astype(BF16)
    vt_ref[0, 0] = _dot_nt(wvt_ref[0], mn).astype(BF16)


def _mem_kv(mem, mem_norm, w_mem_kv):
    B, M, D = mem.shape
    L = mem_norm.shape[0]
    wk = w_mem_kv[:, :, :MEM_W].astype(BF16)
    wvt = jnp.swapaxes(w_mem_kv[:, :, MEM_W:], 1, 2).astype(BF16)
    return pl.pallas_call(
        _memkv_kernel,
        grid=(L, B),
        in_specs=[pl.BlockSpec((1, M, D), lambda l, b: (b, 0, 0)),
                  pl.BlockSpec((1, 1, D), lambda l, b: (l, 0, 0)),
                  pl.BlockSpec((1, D, MEM_W), lambda l, b: (l, 0, 0)),
                  pl.BlockSpec((1, MEM_W, D), lambda l, b: (l, 0, 0))],
        out_specs=[pl.BlockSpec((1, 1, M, MEM_W), lambda l, b: (l, b, 0, 0)),
                   pl.BlockSpec((1, 1, MEM_W, M), lambda l, b: (l, b, 0, 0))],
        out_shape=[jax.ShapeDtypeStruct((L, B, M, MEM_W), BF16), jax.ShapeDtypeStruct((L, B, MEM_W, M), BF16)],
        compiler_params=_cparams(("arbitrary", "arbitrary")),
        name="mem_kv",
    )(mem, mem_norm.reshape(L, 1, D), wk, wvt)


E_CQ, E_CKV, E_KR, E_KRS, E_GMLA, E_QSWA, E_KSWA, E_GSWA, E_QMEM, E_GMEM, E_END = (
    0, 256, 384, 512, 640, 1152, 1664, 1792, 2304, 2816, 3328)


def _store_lane_tiles(ref, val):
    for c in range(val.shape[1] // LANES):
        ref[c] = val[:, c * LANES:(c + 1) * LANES]


def _in_even_kernel(x_ref, gpre_ref, w_ref, wvs_ref, qn_ref, kvn_ref, wq_ref, wqs_ref, wk_ref, wv_ref, c_ref, s_ref,
                    qmla_ref, kmla_ref, vmla_ref, gmla_ref, qswa_ref, kswa_ref, vswa_ref, gswa_ref,
                    qmem_ref, gmem_ref):
    h = _rms(x_ref[...], gpre_ref[...]).astype(BF16)

    def proj(lo, hi):
        return _dot(h, w_ref[:, lo:hi])

    cos = c_ref[...]
    sin = s_ref[...]
    cos8 = jnp.concatenate([cos] * MLA_HEADS, axis=1)
    sin8 = jnp.concatenate([sin] * MLA_HEADS, axis=1)

    cq = _rms(proj(E_CQ, E_CKV), qn_ref[...]).astype(BF16)
    q = _dot(cq, wq_ref[...]) * cos8 + _dot(cq, wqs_ref[...]) * sin8
    qmla_ref[...] = (q * (LOG2E * (MLA_NOPE + MLA_ROPE) ** -0.5)).astype(BF16)

    ckv = _rms(proj(E_CKV, E_KR), kvn_ref[...]).astype(BF16)
    kpe = proj(E_KR, E_KRS) * cos + proj(E_KRS, E_GMLA) * sin
    kmla_ref[...] = (_dot(ckv, wk_ref[...]) + jnp.concatenate([kpe] * MLA_HEADS, axis=1)).astype(BF16)
    vmla_ref[0] = _dot_nt(wv_ref[...], ckv).astype(BF16)

    gmla_ref[...] = proj(E_GMLA, E_QSWA).astype(BF16)
    qswa_ref[...] = (proj(E_QSWA, E_KSWA) * (LOG2E * HEAD_DIM ** -0.5)).astype(BF16)
    kswa_ref[...] = proj(E_KSWA, E_GSWA).astype(BF16)
    _store_lane_tiles(vswa_ref, _dot_nt(wvs_ref[...], h).astype(BF16))
    gswa_ref[...] = proj(E_GSWA, E_QMEM).astype(BF16)
    qmem_ref[...] = (proj(E_QMEM, E_GMEM) * (LOG2E * MEM_HEAD_DIM ** -0.5)).astype(BF16)
    gmem_ref[...] = proj(E_GMEM, E_END).astype(BF16)


def _perm_heads(w, axis):
    idx = np.concatenate([np.arange(h * HEAD_DIM, (h + 1) * HEAD_DIM) for h in HEAD_ORDER])
    return jnp.take(w, idx, axis=axis)


def _even_weights(w_in, w_uq, w_ukv):
    D = w_in.shape[0]
    z = lambda n: jnp.zeros((D, n), F32)
    c = lambda lo, hi: w_in[:, lo:hi]
    w = jnp.concatenate([
        c(0, 256), c(256, 384),
        z(MLA_NOPE), c(384, 416), z(32),
        z(MLA_NOPE), c(400, 416), c(384, 400), z(32),
        c(416, 928),
        _perm_heads(c(928, 1440), 1), c(1440, 1568),
        _perm_heads(c(1696, 2208), 1),
        c(2208, 2720), c(2720, 3232)], axis=1).astype(BF16)
    wvs_t = c(1568, 1696).T.astype(BF16)
    hq = MLA_NOPE + MLA_ROPE
    zq = lambda n: jnp.zeros((MLA_Q_RANK, n), F32)
    wq = jnp.concatenate([jnp.concatenate([w_uq[:, h * hq:(h + 1) * hq], zq(32)], axis=1)
                          for h in range(MLA_HEADS)], axis=1).astype(BF16)
    wqs = jnp.concatenate([jnp.concatenate([zq(MLA_NOPE),
                                            w_uq[:, h * hq + MLA_NOPE + ROPE_HALF:(h + 1) * hq],
                                            w_uq[:, h * hq + MLA_NOPE:h * hq + MLA_NOPE + ROPE_HALF],
                                            zq(32)], axis=1)
                           for h in range(MLA_HEADS)], axis=1).astype(BF16)
    hk = MLA_NOPE + MLA_V
    zk = jnp.zeros((MLA_KV_RANK, LANES - MLA_NOPE), F32)
    wk = jnp.concatenate([jnp.concatenate([w_ukv[:, h * hk:h * hk + MLA_NOPE], zk], axis=1)
                          for h in range(MLA_HEADS)], axis=1).astype(BF16)
    wv_t = jnp.concatenate([w_ukv[:, h * hk + MLA_NOPE:(h + 1) * hk] for h in range(MLA_HEADS)],
                           axis=1).T.astype(BF16)
    return w, wvs_t, wq, wqs, wk, wv_t


def _rope_tables(pos):
    inv = ROPE_BASE ** (-jnp.arange(ROPE_HALF, dtype=F32) / ROPE_HALF)
    ang = pos.astype(F32)[:, None] * inv[None, :]
    cos, sin = jnp.cos(ang), jnp.sin(ang)
    S = pos.shape[0]
    ones, z32 = jnp.ones((S, MLA_NOPE), F32), jnp.zeros((S, 32), F32)
    ctab = jnp.concatenate([ones, cos, cos, z32], axis=1)
    stab = jnp.concatenate([jnp.zeros((S, MLA_NOPE), F32), -sin, sin, z32], axis=1)
    return ctab, stab


def _full(shape):
    return pl.BlockSpec(shape, lambda i: (0,) * len(shape))


def _in_even(x2, gpre, w, wvs_t, qn, kvn, wq, wqs, wk, wv, ctab, stab):
    T, D = x2.shape
    S = ctab.shape[0]
    tm = PROJ_TM
    ns = S // tm
    row = lambda n: pl.BlockSpec((tm, n), lambda i: (i, 0))
    widths = [1024, 1024, 512, 512, 512, 128, 128, 512, 512, 512]
    out_specs = [row(n) for n in widths]
    out_shape = [jax.ShapeDtypeStruct((T, n), BF16) for n in widths]
    out_specs[2] = pl.BlockSpec((1, widths[2], tm), lambda i: (i, 0, 0))
    out_shape[2] = jax.ShapeDtypeStruct((T // tm, widths[2], tm), BF16)
    out_specs[6] = pl.BlockSpec((tm // LANES, LANES, LANES), lambda i: (i, 0, 0))
    out_shape[6] = jax.ShapeDtypeStruct((T // LANES, LANES, LANES), BF16)
    return pl.pallas_call(
        _in_even_kernel,
        grid=(T // tm,),
        in_specs=[row(D), _full((1, D)), _full(w.shape), _full(wvs_t.shape), _full((1, MLA_Q_RANK)),
                  _full((1, MLA_KV_RANK)), _full(wq.shape), _full(wqs.shape), _full(wk.shape), _full(wv.shape),
                  pl.BlockSpec((tm, LANES), lambda i: (i % ns, 0)),
                  pl.BlockSpec((tm, LANES), lambda i: (i % ns, 0))],
        out_specs=out_specs,
        out_shape=out_shape,
        compiler_params=_cparams(("arbitrary",)),
        name="in_proj_even",
    )(x2, gpre, w, wvs_t, qn, kvn, wq, wqs, wk, wv, ctab, stab)


def _mla_kernel(q_ref, k_ref, vt_ref, o_ref, s_scr, m_scr, acc_scr, *, tq):
    i = pl.program_id(2)
    m_scr[...] = jnp.full(m_scr.shape, NEG_INF, F32)
    acc_scr[...] = jnp.zeros(acc_scr.shape, F32)
    ones = jnp.ones((BF16_ROWS, tq), BF16)

    def scores(j, slot):
        start = pl.multiple_of(j * tq, tq)
        for t in range(2):
            s_scr[slot, t] = _dot_nt(k_ref[0, pl.ds(start, tq), t * LANES:(t + 1) * LANES],
                                     q_ref[0, :, t * LANES:(t + 1) * LANES])

    def accumulate(j, slot, diag):
        for t in range(2):
            s = s_scr[slot, t]
            if diag:
                r = lax.broadcasted_iota(jnp.int32, (tq, tq), 0)
                c = lax.broadcasted_iota(jnp.int32, (tq, tq), 1)
                s = jnp.where(r <= c, s, NEG_INF)
            m_old = m_scr[t]
            m_new = jnp.maximum(m_old, jnp.max(s, axis=0, keepdims=True))
            alpha = jnp.exp2(m_old - m_new)
            p = jnp.exp2(s - m_new).astype(BF16)
            m_scr[t] = m_new
            vt = jnp.concatenate([vt_ref[j, t * MLA_V:(t + 1) * MLA_V, :], ones], axis=0)
            acc_scr[t] = alpha * acc_scr[t] + _dot(vt, p)

    def pair(jj, carry):
        scores(2 * jj + 1, 1)
        accumulate(2 * jj, 0, False)
        scores(2 * jj + 2, 0)
        accumulate(2 * jj + 1, 1, False)
        return carry

    scores(0, 0)
    lax.fori_loop(0, i // 2, pair, 0)

    @pl.when(i % 2 == 0)
    def _():
        accumulate(i, 0, True)

    @pl.when(i % 2 == 1)
    def _():
        scores(i, 1)
        accumulate(i - 1, 0, False)
        accumulate(i, 1, True)

    out_t = jnp.concatenate([acc_scr[t, :MLA_V] / acc_scr[t, MLA_V:MLA_V + 1] for t in range(2)], axis=0)
    o_ref[0] = out_t.T.astype(BF16)


def _mla_attention(q, k, vt):
    B, S, _ = q.shape
    tq = vt.shape[-1]
    nk = S // tq
    n_pairs = MLA_HEADS // 2
    return pl.pallas_call(
        functools.partial(_mla_kernel, tq=tq),
        grid=(B, n_pairs, S // tq),
        in_specs=[pl.BlockSpec((1, tq, 2 * LANES), lambda b, p, i: (b, i, p)),
                  pl.BlockSpec((1, S, 2 * LANES), lambda b, p, i: (b, 0, p)),
                  pl.BlockSpec((nk, 2 * MLA_V, tq), lambda b, p, i: (b, p, 0))],
        out_specs=pl.BlockSpec((1, tq, LANES), lambda b, p, i: (b, i, p)),
        out_shape=jax.ShapeDtypeStruct((B, S, MLA_HEADS * MLA_V), BF16),
        scratch_shapes=[pltpu.VMEM((2, 2, tq, tq), F32), pltpu.VMEM((2, 1, tq), F32),
                        pltpu.VMEM((2, ACC_ROWS, tq), F32)],
        compiler_params=_cparams(("arbitrary", "arbitrary", "arbitrary")),
        name="mla_flash",
    )(q, k, vt)


def _local_kernel(*refs, nkb, window, use_sink, tq):
    refs = list(refs)
    sink_ref = refs.pop(0) if use_sink else None
    q_ref = refs.pop(0)
    k_refs = [refs.pop(0) for _ in range(nkb)]
    vt_refs = [refs.pop(0) for _ in range(nkb)]
    bias_ref, qm_ref, km_ref, vmt_ref, o_ref, om_ref = refs

    i = pl.program_id(0)
    pad = (nkb - 1) * tq
    kl = nkb * tq
    k = jnp.concatenate([r[0] for r in k_refs], axis=0)
    vt = jnp.concatenate([r[0] for r in vt_refs], axis=1)
    s = _dot_nt(k, _split_groups(q_ref, tq)) + bias_ref[0]
    r = lax.broadcasted_iota(jnp.int32, (kl, tq), 0)
    c = lax.broadcasted_iota(jnp.int32, (kl, tq), 1)
    delta = c + pad - r
    mask = ((r + (i * tq - pad)) >= 0) & (delta >= 0) & (delta < window)
    mask_add = jnp.where(mask, 0.0, NEG_INF)
    ones = jnp.ones((BF16_ROWS, kl), BF16)

    out_t = []
    for t in range(N_KV):
        ps, ms = [], []
        for p in range(GROUP):
            h = t * GROUP + p
            sp = s[:, h * tq:(h + 1) * tq] + mask_add
            m = jnp.max(sp, axis=0, keepdims=True)
            if use_sink:
                m = jnp.maximum(m, sink_ref[h])
            ps.append(jnp.exp2(sp - m).astype(BF16))
            ms.append(m)
        vg = jnp.concatenate([vt[t * HEAD_DIM:(t + 1) * HEAD_DIM, :], ones], axis=0)
        acc = _dot(vg, jnp.concatenate(ps, axis=1))
        for p in range(GROUP):
            den = acc[HEAD_DIM:HEAD_DIM + 1, p * tq:(p + 1) * tq]
            if use_sink:
                den = den + jnp.exp2(sink_ref[t * GROUP + p] - ms[p])
            out_t.append(acc[:HEAD_DIM, p * tq:(p + 1) * tq] / den)
    o_ref[0] = _merge_heads(out_t).astype(BF16)

    mem_blocks = []
    for h in range(MEM_HEADS):
        sl = slice(h * MEM_HEAD_DIM, (h + 1) * MEM_HEAD_DIM)
        sm = _dot_nt(km_ref[0, :, sl], qm_ref[0, :, sl])
        pr = jnp.exp2(sm - jnp.max(sm, axis=0, keepdims=True))
        om = _dot(vmt_ref[0, sl, :], pr.astype(BF16)) / jnp.sum(pr, axis=0, keepdims=True)
        mem_blocks.append(om.T)
    om_ref[0] = jnp.concatenate(mem_blocks, axis=1).astype(BF16)


def _local_attention(q, k, vt, bias, qm, km, vmt, window, sinks=None):
    B, S, W = q.shape
    tq = BAND_TQ
    nkb = -(-(window - 1) // tq) + 1
    nt = S // tq
    M = km.shape[1]
    use_sink = sinks is not None

    def k_spec(back):
        return pl.BlockSpec((1, tq, LANES), lambda i, b: (b, jnp.maximum(i - back, 0), 0))

    def vt_spec(back):
        return pl.BlockSpec((1, LANES, tq), lambda i, b: (b * nt + jnp.maximum(i - back, 0), 0, 0))

    in_specs = ([pl.BlockSpec(memory_space=pltpu.SMEM)] if use_sink else []) + (
        [pl.BlockSpec((1, tq, W), lambda i, b: (b, i, 0))]
        + [k_spec(nkb - 1 - n) for n in range(nkb)] + [vt_spec(nkb - 1 - n) for n in range(nkb)]
        + [pl.BlockSpec((1, nkb * tq, N_HEADS * tq), lambda i, b: (i, 0, 0)),
           pl.BlockSpec((1, tq, MEM_W), lambda i, b: (b, i, 0)),
           pl.BlockSpec((1, M, MEM_W), lambda i, b: (b, 0, 0)),
           pl.BlockSpec((1, MEM_W, M), lambda i, b: (b, 0, 0))])
    args = ([sinks] if use_sink else []) + [q] + [k] * nkb + [vt] * nkb + [bias, qm, km, vmt]
    return pl.pallas_call(
        functools.partial(_local_kernel, nkb=nkb, window=window, use_sink=use_sink, tq=tq),
        grid=(nt, B),
        in_specs=in_specs,
        out_specs=[pl.BlockSpec((1, tq, W), lambda i, b: (b, i, 0)),
                   pl.BlockSpec((1, tq, MEM_W), lambda i, b: (b, i, 0))],
        out_shape=[jax.ShapeDtypeStruct((B, S, W), BF16), jax.ShapeDtypeStruct((B, S, MEM_W), BF16)],
        compiler_params=_cparams(("arbitrary", "arbitrary")),
        name="local_attention_w%d" % window,
    )(*args)


def _gate(o_ref, g_ref):
    return (o_ref[...].astype(F32) * _silu(g_ref[...].astype(F32))).astype(BF16)


def _out_even_kernel(x_ref, o1, g1, o2, g2, o3, g3, w_ref, gpost_ref, y_ref):
    W = o1.shape[1]
    y = (_dot(_gate(o1, g1), w_ref[0:W, :]) + _dot(_gate(o2, g2), w_ref[W:2 * W, :])
         + _dot(_gate(o3, g3), w_ref[2 * W:3 * W, :]))
    y_ref[...] = x_ref[...] + _rms(y, gpost_ref[...])


def _out_even(x2, o1, g1, o2, g2, o3, g3, w, gpost):
    T, D = x2.shape
    tm = PROJ_TM
    row = lambda n: pl.BlockSpec((tm, n), lambda i: (i, 0))
    return pl.pallas_call(
        _out_even_kernel,
        grid=(T // tm,),
        in_specs=[row(D)] + [row(512)] * 6 + [_full(w.shape), _full((1, D))],
        out_specs=row(D),
        out_shape=jax.ShapeDtypeStruct((T, D), F32),
        compiler_params=_cparams(("arbitrary",)),
        name="out_proj_even",
    )(x2, o1, g1, o2, g2, o3, g3, w, gpost)


def _out_odd_kernel(x_ref, oc, os_, ow, gl_ref, gn, om, gm, ex_ref, w_ref, gpost_ref, y_ref):
    W = oc.shape[1]
    sig = jax.nn.sigmoid(gl_ref[...])
    hi = sig.astype(BF16)
    lo = (sig - hi.astype(F32)).astype(BF16)
    mix = jnp.zeros(oc.shape, F32)
    for br, o_ref in enumerate((oc, os_, ow)):
        g = _dot(hi, ex_ref[br]) + _dot(lo, ex_ref[br])
        mix = mix + g * o_ref[...].astype(F32)
    nsa = (mix * _silu(gn[...].astype(F32))).astype(BF16)
    y = _dot(nsa, w_ref[0:W, :]) + _dot(_gate(om, gm), w_ref[W:2 * W, :])
    y_ref[...] = x_ref[...] + _rms(y, gpost_ref[...])


def _out_odd(x2, oc, os_, ow, gl, gn, om, gm, ex, w, gpost):
    T, D = x2.shape
    tm = PROJ_TM
    row = lambda n: pl.BlockSpec((tm, n), lambda i: (i, 0))
    return pl.pallas_call(
        _out_odd_kernel,
        grid=(T // tm,),
        in_specs=[row(D), row(512), row(512), row(512), row(LANES), row(512), row(512), row(512),
                  _full(ex.shape), _full(w.shape), _full((1, D))],
        out_specs=row(D),
        out_shape=jax.ShapeDtypeStruct((T, D), F32),
        compiler_params=_cparams(("arbitrary",)),
        name="out_proj_odd",
    )(x2, oc, os_, ow, gl, gn, om, gm, ex, w, gpost)


O_Q, O_KC, O_VC, O_KS, O_KW, O_GL, O_GN, O_QM, O_GM, O_END = (
    0, 512, 640, 768, 896, 1024, 1152, 1664, 2176, 2688)


def _in_odd_kernel(x_ref, gpre_ref, w_ref, wvs_ref, wvw_ref, q_ref, kc_ref, vc_ref, ks_ref, vs_ref, kw_ref, vw_ref,
                   gl_ref, gn_ref, qm_ref, gm_ref):
    h = _rms(x_ref[...], gpre_ref[...]).astype(BF16)

    def proj(lo, hi):
        return _dot(h, w_ref[:, lo:hi])

    q_ref[...] = (proj(O_Q, O_KC) * (LOG2E * HEAD_DIM ** -0.5)).astype(BF16)
    kc_ref[...] = proj(O_KC, O_VC)
    vc_ref[...] = proj(O_VC, O_KS)
    ks_ref[...] = proj(O_KS, O_KW).astype(BF16)
    vs_ref[0] = _dot_nt(wvs_ref[...], h).astype(BF16)
    kw_ref[...] = proj(O_KW, O_GL).astype(BF16)
    _store_lane_tiles(vw_ref, _dot_nt(wvw_ref[...], h).astype(BF16))
    gl_ref[...] = proj(O_GL, O_GN)
    gn_ref[...] = proj(O_GN, O_QM).astype(BF16)
    qm_ref[...] = (proj(O_QM, O_GM) * (LOG2E * MEM_HEAD_DIM ** -0.5)).astype(BF16)
    gm_ref[...] = proj(O_GM, O_END).astype(BF16)


def _odd_weights(w_in):
    D = w_in.shape[0]
    c = lambda lo, hi: w_in[:, lo:hi]
    w = jnp.concatenate([
        _perm_heads(c(0, 512), 1),
        c(512, 640), c(640, 768), c(768, 896), c(1024, 1152),
        c(1280, 1304), jnp.zeros((D, LANES - 3 * N_HEADS), F32),
        _perm_heads(c(1304, 1816), 1),
        c(1816, 2328), c(2328, 2840)], axis=1).astype(BF16)
    return w, c(896, 1024).T.astype(BF16), c(1152, 1280).T.astype(BF16)


def _in_odd(x2, gpre, w, wvs_t, wvw_t):
    T, D = x2.shape
    tm = PROJ_TM
    row = lambda n: pl.BlockSpec((tm, n), lambda i: (i, 0))
    widths = [512, 128, 128, 128, 128, 128, 128, 128, 512, 512, 512]
    dtypes = [BF16, F32, F32, BF16, BF16, BF16, BF16, F32, BF16, BF16, BF16]
    out_specs = [row(n) for n in widths]
    out_shape = [jax.ShapeDtypeStruct((T, n), dt) for n, dt in zip(widths, dtypes)]
    out_specs[4] = pl.BlockSpec((1, LANES, tm), lambda i: (i, 0, 0))
    out_shape[4] = jax.ShapeDtypeStruct((T // tm, LANES, tm), BF16)
    out_specs[6] = pl.BlockSpec((tm // LANES, LANES, LANES), lambda i: (i, 0, 0))
    out_shape[6] = jax.ShapeDtypeStruct((T // LANES, LANES, LANES), BF16)
    return pl.pallas_call(
        _in_odd_kernel,
        grid=(T // tm,),
        in_specs=[row(D), _full((1, D)), _full(w.shape), _full(wvs_t.shape), _full(wvw_t.shape)],
        out_specs=out_specs,
        out_shape=out_shape,
        compiler_params=_cparams(("arbitrary",)),
        name="in_proj_odd",
    )(x2, gpre, w, wvs_t, wvw_t)


def _compress_kernel(rk_ref, rv_ref, pek_ref, pev_ref, w1k_ref, w1v_ref, w2k_ref, w2vt_ref, kc_ref, vct_ref, *, nc):
    def hidden(r_ref, pe_ref, w1_ref):
        r = r_ref[0]
        za = _dot((r + pe_ref[0:1, :]).astype(BF16), w1_ref[0])
        zb = _dot((r + pe_ref[1:2, :]).astype(BF16), w1_ref[1])
        return _silu(za + pltpu.roll(zb, r.shape[0] - 1, 0)).astype(BF16)

    kc = _dot(hidden(rk_ref, pek_ref, w1k_ref), w2k_ref[...])
    kc_ref[0] = jnp.where(lax.broadcasted_iota(jnp.int32, kc.shape, 0) < nc, kc, 0.0).astype(BF16)
    vct = _dot_nt(w2vt_ref[...], hidden(rv_ref, pev_ref, w1v_ref))
    vct_ref[0] = jnp.where(lax.broadcasted_iota(jnp.int32, vct.shape, 1) < nc, vct, 0.0).astype(BF16)


def _compress_weights(pe, w1, w2):
    eye = jnp.eye(N_KV, dtype=F32)
    d = HEAD_DIM
    w1h = w1.reshape(2, CMP_STRIDE, d, CMP_HIDDEN)
    w1x = jnp.einsum('ztdn,gh->ztgdhn', w1h, eye).reshape(2, CMP_STRIDE * N_KV * d, N_KV * CMP_HIDDEN)
    w2x = jnp.einsum('kn,gh->gkhn', w2, eye).reshape(N_KV * CMP_HIDDEN, N_KV * d)
    pex = jnp.broadcast_to(pe.reshape(2, CMP_STRIDE, 1, d), (2, CMP_STRIDE, N_KV, d)).reshape(2, -1)
    return pex, w1x.astype(BF16), w2x.astype(BF16)


def _compress(rk, rv, pek, pev, w1k, w1v, w2k, w2vt):
    B, n, W = rk.shape
    spec3 = lambda a: pl.BlockSpec(a.shape, lambda b: (0,) * a.ndim)
    rspec = pl.BlockSpec((1, n, W), lambda b: (b, 0, 0))
    return pl.pallas_call(
        functools.partial(_compress_kernel, nc=n - 1),
        grid=(B,),
        in_specs=[rspec, rspec, spec3(pek), spec3(pev), spec3(w1k), spec3(w1v), spec3(w2k), spec3(w2vt)],
        out_specs=[pl.BlockSpec((1, n, LANES), lambda b: (b, 0, 0)), pl.BlockSpec((1, LANES, n), lambda b: (b, 0, 0))],
        out_shape=[jax.ShapeDtypeStruct((B, n, LANES), BF16), jax.ShapeDtypeStruct((B, LANES, n), BF16)],
        compiler_params=_cparams(("arbitrary",)),
        name="compress_blocks",
    )(rk, rv, pek, pev, w1k, w1v, w2k, w2vt)


def _cmp_kernel(q_ref, kc_ref, vct_ref, bias_ref, mt_ref, o_ref, sel_ref, *, tq, ns, top_n):
    i = pl.program_id(0)
    n = kc_ref.shape[1]
    s = _dot_nt(kc_ref[0], _split_groups(q_ref, tq)) + bias_ref[0]
    r = lax.broadcasted_iota(jnp.int32, (n, tq), 0)
    q_pos = lax.broadcasted_iota(jnp.int32, (n, tq), 1) + i * tq
    mask = (r * CMP_STRIDE + (CMP_LEN - 1)) <= q_pos

    out_t, psums = [], []
    for t in range(N_KV):
        ps = []
        psum = jnp.zeros((n, tq), F32)
        for p in range(GROUP):
            h = t * GROUP + p
            sp = jnp.where(mask, s[:, h * tq:(h + 1) * tq], NEG_INF)
            m = jnp.max(sp, axis=0, keepdims=True)
            pr = jnp.where(mask, jnp.exp2(sp - m), 0.0)
            pr = pr * (1.0 / jnp.maximum(jnp.sum(pr, axis=0, keepdims=True), 1e-30))
            psum = psum + pr
            ps.append(pr.astype(BF16))
        acc = _dot(vct_ref[0, t * HEAD_DIM:(t + 1) * HEAD_DIM, :], jnp.concatenate(ps, axis=1))
        out_t += [acc[:, p * tq:(p + 1) * tq] for p in range(GROUP)]
        psums.append(psum)
    o_ref[0] = _merge_heads(out_t).astype(BF16)

    blk = lax.broadcasted_iota(jnp.int32, (LANES, tq), 0)
    q_idx = lax.broadcasted_iota(jnp.int32, (LANES, tq), 1) + i * tq
    cur = q_idx // SEL_LEN
    forced = (blk == 0) | (blk == cur) | (blk == cur - 1)
    causal = blk * SEL_LEN <= q_idx
    for t in range(N_KV):
        hi = psums[t].astype(BF16)
        lo = (psums[t] - hi.astype(F32)).astype(BF16)
        imp = _dot(mt_ref[...], hi) + _dot(mt_ref[...], lo)
        score = jnp.where(forced, SEL_FORCE, jnp.where(causal, imp, -SEL_FORCE))
        score = jnp.where(blk < ns, score, LOWEST)
        sel = jnp.zeros((LANES, tq), F32)
        for _ in range(top_n):
            mx = jnp.max(score, axis=0, keepdims=True)
            idx = jnp.min(jnp.where(score == mx, blk, LANES), axis=0, keepdims=True)
            pick = blk == idx
            sel = jnp.where(pick, 1.0, sel)
            score = jnp.where(pick, LOWEST, score)
        sel_ref[0, t] = jnp.where(causal, sel, 0.0).astype(BF16)


def _cmp_to_sel_t(n, ns):
    c_start = np.arange(n)[None, :] * CMP_STRIDE
    s_start = np.arange(LANES)[:, None] * SEL_LEN
    overlap = np.maximum(np.minimum(c_start + CMP_LEN, s_start + SEL_LEN) - np.maximum(c_start, s_start), 0)
    m = overlap.astype(np.float32) / CMP_LEN
    m[:, n - 1:] = 0.0
    m[ns:, :] = 0.0
    return jnp.asarray(m, BF16)


def _cmp_select(q, kc, vct, bias, mt):
    B, S, W = q.shape
    n = kc.shape[1]
    tq = bias.shape[2] // N_HEADS
    ns = S // SEL_LEN
    return pl.pallas_call(
        functools.partial(_cmp_kernel, tq=tq, ns=ns, top_n=min(SEL_TOP_N, ns)),
        grid=(S // tq, B),
        in_specs=[pl.BlockSpec((1, tq, W), lambda i, b: (b, i, 0)),
                  pl.BlockSpec((1, n, LANES), lambda i, b: (b, 0, 0)),
                  pl.BlockSpec((1, LANES, n), lambda i, b: (b, 0, 0)),
                  pl.BlockSpec((1, n, N_HEADS * tq), lambda i, b: (i, 0, 0)),
                  pl.BlockSpec(mt.shape, lambda i, b: (0, 0))],
        out_specs=[pl.BlockSpec((1, tq, W), lambda i, b: (b, i, 0)),
                   pl.BlockSpec((1, N_KV, LANES, tq), lambda i, b: (b, 0, 0, i))],
        out_shape=[jax.ShapeDtypeStruct((B, S, W), BF16),
                   jax.ShapeDtypeStruct((B, N_KV, LANES, S), BF16)],
        compiler_params=_cparams(("arbitrary", "arbitrary")),
        name="cmp_select",
    )(q, kc, vct, bias, mt)


def _slc_kernel(qmin_ref, kmax_ref, tab_ref, q_ref, sel_ref, pq_ref, pk_ref, k_ref, vt_ref, o_ref,
                qs_scr, s_scr, m_scr, acc_scr, *, tq, tk):
    i = pl.program_id(1)
    m_scr[...] = jnp.full(m_scr.shape, NEG_INF, F32)
    acc_scr[...] = jnp.zeros(acc_scr.shape, F32)
    qs_scr[...] = _split_groups(q_ref, tq)
    ones = jnp.ones((BF16_ROWS, tk), BF16)
    gq = GROUP * tq
    last = (i * tq + tq - 1) // tk

    def scores(j, slot):
        start = pl.multiple_of(j * tk, tk)
        k = k_ref[0, pl.ds(start, tk), :]
        for t in range(N_KV):
            s_scr[slot, t] = _dot_nt(k, qs_scr[t * gq:(t + 1) * gq, :])

    def accumulate(j, slot, near, diag):
        blk = lax.broadcasted_iota(jnp.int32, (tk, LANES), 0) // SEL_LEN + j * (tk // SEL_LEN)
        expand = jnp.where(lax.broadcasted_iota(jnp.int32, (tk, LANES), 1) == blk, 1.0, 0.0).astype(BF16)
        if near:
            bucket = _t5_bucket((pq_ref[0] - pk_ref[j]).astype(F32).T)
        if diag:
            k_idx = lax.broadcasted_iota(jnp.int32, (tk, tq), 0) + j * tk
            q_idx = lax.broadcasted_iota(jnp.int32, (tk, tq), 1) + i * tq
            tok_ok = k_idx <= q_idx
        for t in range(N_KV):
            chosen = _dot(expand, sel_ref[0, t]) > 0.5
            if diag:
                chosen = chosen & tok_ok
            mask_add = jnp.where(chosen, 0.0, NEG_INF)
            ps, alphas = [], []
            for p in range(GROUP):
                cols = slice(p * tq, (p + 1) * tq)
                sp = s_scr[slot, t, :, cols] + mask_add
                if near:
                    sp = sp + _bias_lookup(bucket, tab_ref, t * GROUP + p)
                m_old = m_scr[t, :, cols]
                m_new = jnp.maximum(m_old, jnp.max(sp, axis=0, keepdims=True))
                alphas.append(jnp.exp2(m_old - m_new))
                ps.append(jnp.exp2(sp - m_new).astype(BF16))
                m_scr[t, :, cols] = m_new
            vt = jnp.concatenate([vt_ref[j, t * HEAD_DIM:(t + 1) * HEAD_DIM, :], ones], axis=0)
            acc_scr[t] = (jnp.concatenate(alphas, axis=1) * acc_scr[t]
                          + _dot(vt, jnp.concatenate(ps, axis=1)))

    def step(j_next, slot_next, j, slot):
        far = (qmin_ref[i] - kmax_ref[j]) >= FAR_DIST

        @pl.when(far)
        def _():
            scores(j_next, slot_next)
            accumulate(j, slot, False, False)

        @pl.when(jnp.logical_not(far))
        def _():
            scores(j_next, slot_next)
            accumulate(j, slot, True, False)

    def pair(jj, carry):
        step(2 * jj + 1, 1, 2 * jj, 0)
        step(2 * jj + 2, 0, 2 * jj + 1, 1)
        return carry

    scores(0, 0)
    lax.fori_loop(0, last // 2, pair, 0)

    @pl.when(last % 2 == 0)
    def _():
        accumulate(last, 0, True, True)

    @pl.when(last % 2 == 1)
    def _():
        step(last, 1, last - 1, 0)
        accumulate(last, 1, True, True)

    out_t = []
    for t in range(N_KV):
        for p in range(GROUP):
            cols = slice(p * tq, (p + 1) * tq)
            out_t.append(acc_scr[t, :HEAD_DIM, cols] / acc_scr[t, HEAD_DIM:HEAD_DIM + 1, cols])
    o_ref[0] = _merge_heads(out_t).astype(BF16)


def _slc_attention(q, sel, k, vt, pos, tab_rel):
    B, S, W = q.shape
    tk = vt.shape[-1]
    tq = min(SLC_TQ, S)
    nq, nk = S // tq, S // tk
    qmin = pos.reshape(nq, tq).min(axis=1)
    kmax = pos.reshape(nk, tk).max(axis=1)
    smem = pl.BlockSpec(memory_space=pltpu.SMEM)
    return pl.pallas_call(
        functools.partial(_slc_kernel, tq=tq, tk=tk),
        grid=(B, nq),
        in_specs=[smem, smem,
                  pl.BlockSpec(tab_rel.shape, lambda b, i: (0, 0)),
                  pl.BlockSpec((1, tq, W), lambda b, i: (b, i, 0)),
                  pl.BlockSpec((1, N_KV, LANES, tq), lambda b, i: (b, 0, 0, i)),
                  pl.BlockSpec((1, tq, 1), lambda b, i: (i, 0, 0)),
                  pl.BlockSpec((nk, 1, tk), lambda b, i: (0, 0, 0)),
                  pl.BlockSpec((1, S, LANES), lambda b, i: (b, 0, 0)),
                  pl.BlockSpec((nk, LANES, tk), lambda b, i: (b, 0, 0))],
        out_specs=pl.BlockSpec((1, tq, W), lambda b, i: (b, i, 0)),
        out_shape=jax.ShapeDtypeStruct((B, S, W), BF16),
        scratch_shapes=[pltpu.VMEM((N_HEADS * tq, LANES), BF16),
                        pltpu.VMEM((2, N_KV, tk, GROUP * tq), F32),
                        pltpu.VMEM((N_KV, 1, GROUP * tq), F32),
                        pltpu.VMEM((N_KV, ACC_ROWS, GROUP * tq), F32)],
        compiler_params=_cparams(("arbitrary", "arbitrary")),
        name="slc_attention",
    )(qmin, kmax, tab_rel, q, sel, pos.reshape(nq, tq, 1), pos.reshape(nk, 1, tk), k, vt)


def _band_positions(pos, tq, pad):
    S = pos.shape[0]
    nq = S // tq
    posp = jnp.pad(pos, (pad, 0))
    win = posp[(jnp.arange(nq) * tq)[:, None] + jnp.arange(pad + tq)[None, :]]
    return pos.reshape(nq, 1, tq), win.reshape(nq, pad + tq, 1)


def _bias_table(table):
    t = jnp.zeros((N_HEADS, LANES), F32)
    return t.at[:, :NUM_BUCKETS].set(table.astype(F32).T * LOG2E)


def _gate_expand():
    ex = np.zeros((3, LANES, N_HEADS * HEAD_DIM), np.float32)
    for slot, h in enumerate(HEAD_ORDER):
        for br in range(3):
            ex[br, 3 * h + br, slot * HEAD_DIM:(slot + 1) * HEAD_DIM] = 1.0
    return jnp.asarray(ex, BF16)


def kernel(x, mem, positions, rel_bias_table, norm_pre, norm_post, mem_norm, w_mem_kv, w_in_even, mla_q_norm,
           mla_kv_norm, mla_w_uq, mla_w_ukv, swa_sinks, w_out_even, w_in_odd, cmp_pos_k, cmp_pos_v, cmp_w1_k,
           cmp_w2_k, cmp_w1_v, cmp_w2_v, w_out_odd):
    B, S, D = x.shape
    depth = norm_pre.shape[0]
    T = B * S
    pos = positions.astype(jnp.int32)
    n_cmp = S // CMP_STRIDE

    tab = _bias_table(rel_bias_table)
    tab_rel = _bias_table(rel_bias_table - rel_bias_table[NUM_BUCKETS - 1:, :])
    pq, pk_swa = _band_positions(pos, BAND_TQ, BAND_TQ)
    bias_swa = _bias_tiles(tab, pq, pk_swa, True)
    _, pk_win = _band_positions(pos, BAND_TQ, (-(-(NSA_WINDOW - 1) // BAND_TQ)) * BAND_TQ)
    bias_win = _bias_tiles(tab, pq, pk_win, True)
    c_end = jnp.minimum(jnp.arange(n_cmp) * CMP_STRIDE + CMP_LEN - 1, S - 1)
    cmp_tq = min(CMP_TQ, S)
    bias_cmp = _bias_tiles(tab, pos.reshape(S // cmp_tq, 1, cmp_tq), pos[c_end].reshape(1, n_cmp, 1), False)

    ctab, stab = _rope_tables(pos)
    mem_k, mem_vt = _mem_kv(mem, mem_norm, w_mem_kv)
    m_sel_t = _cmp_to_sel_t(n_cmp, S // SEL_LEN)
    gate_ex = _gate_expand()

    x2 = x.reshape(T, D)
    r3 = lambda a: a.reshape(B, S, a.shape[-1])
    for l in range(depth):
        gpre = norm_pre[l].reshape(1, D)
        gpost = norm_post[l].reshape(1, D)
        if l % 2 == 0:
            e = l // 2
            w, wvs_t, wq, wqs, wk, wv = _even_weights(w_in_even[e], mla_w_uq[e], mla_w_ukv[e])
            (qmla, kmla, vmla_t, gmla, qswa, kswa, vswa_t, gswa, qmem, gmem) = _in_even(
                x2, gpre, w, wvs_t, mla_q_norm[e].reshape(1, -1), mla_kv_norm[e].reshape(1, -1),
                wq, wqs, wk, wv, ctab, stab)
            o_mla = _mla_attention(r3(qmla), r3(kmla), vmla_t)
            o_swa, o_mem = _local_attention(r3(qswa), r3(kswa), vswa_t, bias_swa, r3(qmem),
                                            mem_k[l], mem_vt[l], SWA_WINDOW, sinks=swa_sinks[e] * LOG2E)
            w_o = w_out_even[e]
            w_o = jnp.concatenate([w_o[:512], _perm_heads(w_o[512:1024], 0), w_o[1024:]], axis=0).astype(BF16)
            x2 = _out_even(x2, o_mla.reshape(T, -1), gmla, o_swa.reshape(T, -1), gswa,
                           o_mem.reshape(T, -1), gmem, w_o, gpost)
        else:
            o = l // 2
            w, wvs_t, wvw_t = _odd_weights(w_in_odd[o])
            (q, kc_in, vc_in, ks, vs_t, kw, vw_t, gl, gn, qmem, gmem) = _in_odd(x2, gpre, w, wvs_t, wvw_t)
            pek, w1k, w2k = _compress_weights(cmp_pos_k[o], cmp_w1_k[o], cmp_w2_k[o])
            pev, w1v, w2v = _compress_weights(cmp_pos_v[o], cmp_w1_v[o], cmp_w2_v[o])
            chunks = lambda a: a.reshape(B, n_cmp, CMP_STRIDE * LANES)
            kc, vc_t = _compress(chunks(kc_in), chunks(vc_in), pek, pev, w1k, w1v, w2k, w2v.T)
            o_cmp, sel = _cmp_select(r3(q), kc, vc_t, bias_cmp, m_sel_t)
            o_slc = _slc_attention(r3(q), sel, r3(ks), vs_t, pos, tab_rel)
            o_win, o_mem = _local_attention(r3(q), r3(kw), vw_t, bias_win, r3(qmem),
                                            mem_k[l], mem_vt[l], NSA_WINDOW)
            w_o = w_out_odd[o]
            w_o = jnp.concatenate([_perm_heads(w_o[:512], 0), w_o[512:]], axis=0).astype(BF16)
            x2 = _out_odd(x2, o_cmp.reshape(T, -1), o_slc.reshape(T, -1), o_win.reshape(T, -1), gl, gn,
                          o_mem.reshape(T, -1), gmem, gate_ex, w_o, gpost)
    return x2.reshape(B, S, D)
```

```python
import functools
import math

import numpy as np
import jax
import jax.numpy as jnp
from jax import lax
from jax.experimental import pallas as pl
from jax.experimental.pallas import tpu as pltpu

F32 = jnp.float32
BF16 = jnp.bfloat16

NEG_INF = -1e30
EPS = 1e-6
SEL_FORCE = 1e9
LOWEST = -3.0e38
LOG2E = math.log2(math.e)

NUM_BUCKETS = 32
REL_MAX_DIST = 128

MLA_HEADS = 8
MLA_Q_RANK = 256
MLA_KV_RANK = 128
MLA_NOPE = 64
MLA_ROPE = 32
MLA_V = 64
ROPE_BASE = 10000.0
ROPE_HALF = MLA_ROPE // 2

HEAD_DIM = 64
N_HEADS = 8
N_KV = 2
GROUP = N_HEADS // N_KV
SWA_WINDOW = 128
NSA_WINDOW = 512
CMP_LEN = 32
CMP_STRIDE = 16
CMP_HIDDEN = 256
SEL_LEN = 64
SEL_TOP_N = 16
MEM_HEADS = 4
MEM_HEAD_DIM = 128
MEM_W = MEM_HEADS * MEM_HEAD_DIM

LANES = 128
BF16_ROWS = 16
ACC_ROWS = HEAD_DIM + BF16_ROWS
BAND_TQ = 128
CMP_TQ = 256
MLA_TQ = 512
SLC_TQ = 256
SLC_TK = 512
PROJ_TM = 512
VMEM_LIMIT = 56 * 1024 * 1024


def _t5_far_distance():
    max_exact = NUM_BUCKETS // 2
    d = np.arange(0, 2 * REL_MAX_DIST)
    x = (np.log(np.maximum(d, 1).astype(np.float64) / max_exact)
         / math.log(REL_MAX_DIST / max_exact) * (NUM_BUCKETS - max_exact))
    assert np.abs(x - np.round(x))[max_exact + 1:REL_MAX_DIST].min() > 1e-3
    bucket = np.where(d < max_exact, d, np.minimum(max_exact + np.trunc(x).astype(np.int64), NUM_BUCKETS - 1))
    assert (np.diff(bucket) >= 0).all()
    return int(np.argmax(bucket >= NUM_BUCKETS - 1))


FAR_DIST = _t5_far_distance()

HEAD_ORDER = [h for p in range(GROUP) for h in (p, GROUP + p)]


def _cparams(sem):
    return pltpu.CompilerParams(dimension_semantics=sem, vmem_limit_bytes=VMEM_LIMIT)


def _dot(a, b):
    return jnp.dot(a, b, preferred_element_type=F32)


def _dot_nt(a, b):
    return lax.dot_general(a, b, (((1,), (1,)), ((), ())), preferred_element_type=F32)


def _rms(x, g):
    return x * lax.rsqrt(jnp.mean(x * x, axis=-1, keepdims=True) + EPS) * g


def _silu(x):
    return x * jax.nn.sigmoid(x)


def _t5_bucket(dist):
    max_exact = NUM_BUCKETS // 2
    d = jnp.maximum(dist, 0.0)
    scale = (NUM_BUCKETS - max_exact) / math.log2(REL_MAX_DIST / max_exact)
    large = max_exact + jnp.floor(jnp.log2(jnp.maximum(d, 1.0) * (1.0 / max_exact)) * scale)
    return jnp.where(d < max_exact, d, jnp.minimum(large, NUM_BUCKETS - 1.0)).astype(jnp.int32)


def _bias_lookup(bucket, tab_ref, head):
    rows = jnp.broadcast_to(tab_ref[head:head + 1, :], (bucket.shape[0], LANES))
    return jnp.concatenate(
        [jnp.take_along_axis(rows, bucket[:, c * LANES:(c + 1) * LANES], axis=1)
         for c in range(bucket.shape[1] // LANES)], axis=1)


def _split_groups(q_ref, tq):
    first = lax.broadcasted_iota(jnp.int32, (tq, LANES), 1) < HEAD_DIM
    parts = []
    for t in range(N_KV):
        for p in range(GROUP):
            qp = q_ref[0, :, p * LANES:(p + 1) * LANES]
            parts.append(jnp.where(first if t == 0 else ~first, qp, jnp.zeros_like(qp)))
    return jnp.concatenate(parts, axis=0)


def _merge_heads(out_t):
    blocks = [jnp.concatenate([out_t[p], out_t[GROUP + p]], axis=0).T for p in range(GROUP)]
    return jnp.concatenate(blocks, axis=1)


def _band_visible(i, kl, tq, window):
    pad = kl - tq
    r = lax.broadcasted_iota(jnp.int32, (kl, tq), 0)
    c = lax.broadcasted_iota(jnp.int32, (kl, tq), 1)
    delta = c + pad - r
    return ((r + (i * tq - pad)) >= 0) & (delta >= 0) & (delta < window)


def _cmp_visible(i, n, tq):
    r = lax.broadcasted_iota(jnp.int32, (n, tq), 0)
    c = lax.broadcasted_iota(jnp.int32, (n, tq), 1)
    return (r * CMP_STRIDE + (CMP_LEN - 1)) <= (c + i * tq)


def _bias_kernel(tab_ref, pq_ref, pk_ref, o_ref, *, tq, window):
    i = pl.program_id(0)
    kl = pk_ref.shape[1]
    bucket = _t5_bucket((pq_ref[0] - pk_ref[0]).astype(F32))
    visible = _cmp_visible(i, kl, tq) if window is None else _band_visible(i, kl, tq, window)
    mask_add = jnp.where(visible, 0.0, NEG_INF)
    for h in range(N_HEADS):
        o_ref[0, :, h * tq:(h + 1) * tq] = _bias_lookup(bucket, tab_ref, h) + mask_add


def _bias_tiles(tab, pos_q, pos_k, window):
    nq, _, tq = pos_q.shape
    kl = pos_k.shape[1]
    per_tile_keys = window is not None
    return pl.pallas_call(
        functools.partial(_bias_kernel, tq=tq, window=window),
        grid=(nq,),
        in_specs=[pl.BlockSpec(tab.shape, lambda i: (0, 0)),
                  pl.BlockSpec((1, 1, tq), lambda i: (i, 0, 0)),
                  pl.BlockSpec((1, kl, 1), (lambda i: (i, 0, 0)) if per_tile_keys else (lambda i: (0, 0, 0)))],
        out_specs=pl.BlockSpec((1, kl, N_HEADS * tq), lambda i: (i, 0, 0)),
        out_shape=jax.ShapeDtypeStruct((nq, kl, N_HEADS * tq), F32),
        compiler_params=_cparams(("arbitrary",)),
        name="bias_tiles",
    )(tab, pos_q, pos_k)


def _memkv_kernel(mem_ref, g_ref, wk_ref, wvt_ref, k_ref, vt_ref):
    mn = _rms(mem_ref[0], g_ref[0]).astype(BF16)
    k_ref[0, 0] = _dot(mn, wk_ref[0]).astype(BF16)
    vt_ref[0, 0] = _dot_nt(wvt_ref[0], mn).astype(BF16)


def _mem_kv(mem, mem_norm, w_mem_kv):
    B, M, D = mem.shape
    L = mem_norm.shape[0]
    wk = w_mem_kv[:, :, :MEM_W].astype(BF16)
    wvt = jnp.swapaxes(w_mem_kv[:, :, MEM_W:], 1, 2).astype(BF16)
    return pl.pallas_call(
        _memkv_kernel,
        grid=(L, B),
        in_specs=[pl.BlockSpec((1, M, D), lambda l, b: (b, 0, 0)),
                  pl.BlockSpec((1, 1, D), lambda l, b: (l, 0, 0)),
                  pl.BlockSpec((1, D, MEM_W), lambda l, b: (l, 0, 0)),
                  pl.BlockSpec((1, MEM_W, D), lambda l, b: (l, 0, 0))],
        out_specs=[pl.BlockSpec((1, 1, M, MEM_W), lambda l, b: (l, b, 0, 0)),
                   pl.BlockSpec((1, 1, MEM_W, M), lambda l, b: (l, b, 0, 0))],
        out_shape=[jax.ShapeDtypeStruct((L, B, M, MEM_W), BF16), jax.ShapeDtypeStruct((L, B, MEM_W, M), BF16)],
        compiler_params=_cparams(("arbitrary", "arbitrary")),
        name="mem_kv",
    )(mem, mem_norm.reshape(L, 1, D), wk, wvt)


E_CQ, E_CKV, E_KR, E_KRS, E_GMLA, E_QSWA, E_KSWA, E_GSWA, E_QMEM, E_GMEM, E_END = (
    0, 256, 384, 512, 640, 1152, 1664, 1792, 2304, 2816, 3328)


def _store_lane_tiles(ref, val):
    for c in range(val.shape[1] // LANES):
        ref[c] = val[:, c * LANES:(c + 1) * LANES]


def _in_even_kernel(x_ref, gpre_ref, w_ref, wvs_ref, qn_ref, kvn_ref, wq_ref, wqs_ref, wk_ref, wv_ref, c_ref, s_ref,
                    qmla_ref, kmla_ref, vmla_ref, gmla_ref, qswa_ref, kswa_ref, vswa_ref, gswa_ref,
                    qmem_ref, gmem_ref):
    h = _rms(x_ref[...], gpre_ref[...]).astype(BF16)

    def proj(lo, hi):
        return _dot(h, w_ref[:, lo:hi])

    cos = c_ref[...]
    sin = s_ref[...]
    cos8 = jnp.concatenate([cos] * MLA_HEADS, axis=1)
    sin8 = jnp.concatenate([sin] * MLA_HEADS, axis=1)

    cq = _rms(proj(E_CQ, E_CKV), qn_ref[...]).astype(BF16)
    q = _dot(cq, wq_ref[...]) * cos8 + _dot(cq, wqs_ref[...]) * sin8
    qmla_ref[...] = (q * (LOG2E * (MLA_NOPE + MLA_ROPE) ** -0.5)).astype(BF16)

    ckv = _rms(proj(E_CKV, E_KR), kvn_ref[...]).astype(BF16)
    kpe = proj(E_KR, E_KRS) * cos + proj(E_KRS, E_GMLA) * sin
    kmla_ref[...] = (_dot(ckv, wk_ref[...]) + jnp.concatenate([kpe] * MLA_HEADS, axis=1)).astype(BF16)
    vmla_ref[0] = _dot_nt(wv_ref[...], ckv).astype(BF16)

    gmla_ref[...] = proj(E_GMLA, E_QSWA).astype(BF16)
    qswa_ref[...] = (proj(E_QSWA, E_KSWA) * (LOG2E * HEAD_DIM ** -0.5)).astype(BF16)
    kswa_ref[...] = proj(E_KSWA, E_GSWA).astype(BF16)
    _store_lane_tiles(vswa_ref, _dot_nt(wvs_ref[...], h).astype(BF16))
    gswa_ref[...] = proj(E_GSWA, E_QMEM).astype(BF16)
    qmem_ref[...] = (proj(E_QMEM, E_GMEM) * (LOG2E * MEM_HEAD_DIM ** -0.5)).astype(BF16)
    gmem_ref[...] = proj(E_GMEM, E_END).astype(BF16)


def _perm_heads(w, axis):
    w = jnp.moveaxis(w, axis, 0)
    rest = w.shape[1:]
    w = w.reshape((N_KV, GROUP, HEAD_DIM) + rest).swapaxes(0, 1).reshape((N_HEADS * HEAD_DIM,) + rest)
    return jnp.moveaxis(w, 0, axis)


def _even_weights(w_in, w_uq, w_ukv):
    D = w_in.shape[0]
    z = lambda n: jnp.zeros((D, n), F32)
    c = lambda lo, hi: w_in[:, lo:hi]
    w = jnp.concatenate([
        c(0, 256), c(256, 384),
        z(MLA_NOPE), c(384, 416), z(32),
        z(MLA_NOPE), c(400, 416), c(384, 400), z(32),
        c(416, 928),
        _perm_heads(c(928, 1440), 1), c(1440, 1568),
        _perm_heads(c(1696, 2208), 1),
        c(2208, 2720), c(2720, 3232)], axis=1).astype(BF16)
    wvs_t = c(1568, 1696).T.astype(BF16)
    hq = MLA_NOPE + MLA_ROPE
    zq = lambda n: jnp.zeros((MLA_Q_RANK, n), F32)
    wq = jnp.concatenate([jnp.concatenate([w_uq[:, h * hq:(h + 1) * hq], zq(32)], axis=1)
                          for h in range(MLA_HEADS)], axis=1).astype(BF16)
    wqs = jnp.concatenate([jnp.concatenate([zq(MLA_NOPE),
                                            w_uq[:, h * hq + MLA_NOPE + ROPE_HALF:(h + 1) * hq],
                                            w_uq[:, h * hq + MLA_NOPE:h * hq + MLA_NOPE + ROPE_HALF],
                                            zq(32)], axis=1)
                           for h in range(MLA_HEADS)], axis=1).astype(BF16)
    hk = MLA_NOPE + MLA_V
    zk = jnp.zeros((MLA_KV_RANK, LANES - MLA_NOPE), F32)
    wk = jnp.concatenate([jnp.concatenate([w_ukv[:, h * hk:h * hk + MLA_NOPE], zk], axis=1)
                          for h in range(MLA_HEADS)], axis=1).astype(BF16)
    wv_t = jnp.concatenate([w_ukv[:, h * hk + MLA_NOPE:(h + 1) * hk] for h in range(MLA_HEADS)],
                           axis=1).T.astype(BF16)
    return w, wvs_t, wq, wqs, wk, wv_t


def _rope_tables(pos):
    inv = ROPE_BASE ** (-jnp.arange(ROPE_HALF, dtype=F32) / ROPE_HALF)
    ang = pos.astype(F32)[:, None] * inv[None, :]
    cos, sin = jnp.cos(ang), jnp.sin(ang)
    S = pos.shape[0]
    ones, z32 = jnp.ones((S, MLA_NOPE), F32), jnp.zeros((S, 32), F32)
    ctab = jnp.concatenate([ones, cos, cos, z32], axis=1)
    stab = jnp.concatenate([jnp.zeros((S, MLA_NOPE), F32), -sin, sin, z32], axis=1)
    return ctab, stab


def _full(shape):
    return pl.BlockSpec(shape, lambda i: (0,) * len(shape))


def _in_even(x2, gpre, w, wvs_t, qn, kvn, wq, wqs, wk, wv, ctab, stab):
    T, D = x2.shape
    S = ctab.shape[0]
    tm = PROJ_TM
    ns = S // tm
    row = lambda n: pl.BlockSpec((tm, n), lambda i: (i, 0))
    widths = [1024, 1024, 512, 512, 512, 128, 128, 512, 512, 512]
    out_specs = [row(n) for n in widths]
    out_shape = [jax.ShapeDtypeStruct((T, n), BF16) for n in widths]
    out_specs[2] = pl.BlockSpec((1, widths[2], tm), lambda i: (i, 0, 0))
    out_shape[2] = jax.ShapeDtypeStruct((T // tm, widths[2], tm), BF16)
    out_specs[6] = pl.BlockSpec((tm // LANES, LANES, LANES), lambda i: (i, 0, 0))
    out_shape[6] = jax.ShapeDtypeStruct((T // LANES, LANES, LANES), BF16)
    return pl.pallas_call(
        _in_even_kernel,
        grid=(T // tm,),
        in_specs=[row(D), _full((1, D)), _full(w.shape), _full(wvs_t.shape), _full((1, MLA_Q_RANK)),
                  _full((1, MLA_KV_RANK)), _full(wq.shape), _full(wqs.shape), _full(wk.shape), _full(wv.shape),
                  pl.BlockSpec((tm, LANES), lambda i: (i % ns, 0)),
                  pl.BlockSpec((tm, LANES), lambda i: (i % ns, 0))],
        out_specs=out_specs,
        out_shape=out_shape,
        compiler_params=_cparams(("arbitrary",)),
        name="in_proj_even",
    )(x2, gpre, w, wvs_t, qn, kvn, wq, wqs, wk, wv, ctab, stab)


def _mla_kernel(q_ref, k_ref, vt_ref, o_ref, s_scr, mx_scr, m_scr, acc_scr, *, tq):
    i = pl.program_id(2)
    m_scr[...] = jnp.full(m_scr.shape, NEG_INF, F32)
    acc_scr[...] = jnp.zeros(acc_scr.shape, F32)
    ones = jnp.ones((BF16_ROWS, tq), BF16)

    def scores(j, slot):
        start = pl.multiple_of(j * tq, tq)
        for t in range(2):
            s = _dot_nt(k_ref[0, pl.ds(start, tq), t * LANES:(t + 1) * LANES],
                        q_ref[0, :, t * LANES:(t + 1) * LANES])
            s_scr[slot, t] = s
            mx_scr[slot, t] = jnp.max(s, axis=0, keepdims=True)

    def accumulate(j, slot, diag):
        for t in range(2):
            s = s_scr[slot, t]
            if diag:
                r = lax.broadcasted_iota(jnp.int32, (tq, tq), 0)
                c = lax.broadcasted_iota(jnp.int32, (tq, tq), 1)
                s = jnp.where(r <= c, s, NEG_INF)
                tile_max = jnp.max(s, axis=0, keepdims=True)
            else:
                tile_max = mx_scr[slot, t]
            m_old = m_scr[t]
            m_new = jnp.maximum(m_old, tile_max)
            alpha = jnp.exp2(m_old - m_new)
            p = jnp.exp2(s - m_new).astype(BF16)
            m_scr[t] = m_new
            vt = jnp.concatenate([vt_ref[j, t * MLA_V:(t + 1) * MLA_V, :], ones], axis=0)
            acc_scr[t] = alpha * acc_scr[t] + _dot(vt, p)

    def pair(jj, carry):
        scores(2 * jj + 1, 1)
        accumulate(2 * jj, 0, False)
        scores(2 * jj + 2, 0)
        accumulate(2 * jj + 1, 1, False)
        return carry

    scores(0, 0)
    lax.fori_loop(0, i // 2, pair, 0)

    @pl.when(i % 2 == 0)
    def _():
        accumulate(i, 0, True)

    @pl.when(i % 2 == 1)
    def _():
        scores(i, 1)
        accumulate(i - 1, 0, False)
        accumulate(i, 1, True)

    out_t = jnp.concatenate([acc_scr[t, :MLA_V] / acc_scr[t, MLA_V:MLA_V + 1] for t in range(2)], axis=0)
    o_ref[0] = out_t.T.astype(BF16)


def _mla_attention(q, k, vt):
    B, S, _ = q.shape
    tq = vt.shape[-1]
    nk = S // tq
    n_pairs = MLA_HEADS // 2
    return pl.pallas_call(
        functools.partial(_mla_kernel, tq=tq),
        grid=(B, n_pairs, S // tq),
        in_specs=[pl.BlockSpec((1, tq, 2 * LANES), lambda b, p, i: (b, i, p)),
                  pl.BlockSpec((1, S, 2 * LANES), lambda b, p, i: (b, 0, p)),
                  pl.BlockSpec((nk, 2 * MLA_V, tq), lambda b, p, i: (b, p, 0))],
        out_specs=pl.BlockSpec((1, tq, LANES), lambda b, p, i: (b, i, p)),
        out_shape=jax.ShapeDtypeStruct((B, S, MLA_HEADS * MLA_V), BF16),
        scratch_shapes=[pltpu.VMEM((2, 2, tq, tq), F32), pltpu.VMEM((2, 2, 1, tq), F32),
                        pltpu.VMEM((2, 1, tq), F32), pltpu.VMEM((2, ACC_ROWS, tq), F32)],
        compiler_params=_cparams(("arbitrary", "arbitrary", "arbitrary")),
        name="mla_flash",
    )(q, k, vt)


def _local_kernel(*refs, nkb, use_sink, tq):
    refs = list(refs)
    sink_ref = refs.pop(0) if use_sink else None
    q_ref = refs.pop(0)
    k_refs = [refs.pop(0) for _ in range(nkb)]
    vt_refs = [refs.pop(0) for _ in range(nkb)]
    bias_ref, qm_ref, km_ref, vmt_ref, o_ref, om_ref = refs

    kl = nkb * tq
    k = jnp.concatenate([r[0] for r in k_refs], axis=0)
    vt = jnp.concatenate([r[0] for r in vt_refs], axis=1)
    s = _dot_nt(k, _split_groups(q_ref, tq)) + bias_ref[0]
    ones = jnp.ones((BF16_ROWS, kl), BF16)

    out_t = []
    for t in range(N_KV):
        ps, ms = [], []
        for p in range(GROUP):
            h = t * GROUP + p
            sp = s[:, h * tq:(h + 1) * tq]
            m = jnp.max(sp, axis=0, keepdims=True)
            if use_sink:
                m = jnp.maximum(m, sink_ref[h])
            ps.append(jnp.exp2(sp - m).astype(BF16))
            ms.append(m)
        vg = jnp.concatenate([vt[t * HEAD_DIM:(t + 1) * HEAD_DIM, :], ones], axis=0)
        acc = _dot(vg, jnp.concatenate(ps, axis=1))
        for p in range(GROUP):
            den = acc[HEAD_DIM:HEAD_DIM + 1, p * tq:(p + 1) * tq]
            if use_sink:
                den = den + jnp.exp2(sink_ref[t * GROUP + p] - ms[p])
            out_t.append(acc[:HEAD_DIM, p * tq:(p + 1) * tq] / den)
    o_ref[0] = _merge_heads(out_t).astype(BF16)

    mem_blocks = []
    for h in range(MEM_HEADS):
        sl = slice(h * MEM_HEAD_DIM, (h + 1) * MEM_HEAD_DIM)
        sm = _dot_nt(km_ref[0, :, sl], qm_ref[0, :, sl])
        pr = jnp.exp2(sm - jnp.max(sm, axis=0, keepdims=True))
        om = _dot(vmt_ref[0, sl, :], pr.astype(BF16)) / jnp.sum(pr, axis=0, keepdims=True)
        mem_blocks.append(om.T)
    om_ref[0] = jnp.concatenate(mem_blocks, axis=1).astype(BF16)


def _local_attention(q, k, vt, bias, qm, km, vmt, window, sinks=None):
    B, S, W = q.shape
    tq = BAND_TQ
    nkb = -(-(window - 1) // tq) + 1
    nt = S // tq
    M = km.shape[1]
    use_sink = sinks is not None

    def k_spec(back):
        return pl.BlockSpec((1, tq, LANES), lambda i, b: (b, jnp.maximum(i - back, 0), 0))

    def vt_spec(back):
        return pl.BlockSpec((1, LANES, tq), lambda i, b: (b * nt + jnp.maximum(i - back, 0), 0, 0))

    in_specs = ([pl.BlockSpec(memory_space=pltpu.SMEM)] if use_sink else []) + (
        [pl.BlockSpec((1, tq, W), lambda i, b: (b, i, 0))]
        + [k_spec(nkb - 1 - n) for n in range(nkb)] + [vt_spec(nkb - 1 - n) for n in range(nkb)]
        + [pl.BlockSpec((1, nkb * tq, N_HEADS * tq), lambda i, b: (i, 0, 0)),
           pl.BlockSpec((1, tq, MEM_W), lambda i, b: (b, i, 0)),
           pl.BlockSpec((1, M, MEM_W), lambda i, b: (b, 0, 0)),
           pl.BlockSpec((1, MEM_W, M), lambda i, b: (b, 0, 0))])
    args = ([sinks] if use_sink else []) + [q] + [k] * nkb + [vt] * nkb + [bias, qm, km, vmt]
    return pl.pallas_call(
        functools.partial(_local_kernel, nkb=nkb, use_sink=use_sink, tq=tq),
        grid=(nt, B),
        in_specs=in_specs,
        out_specs=[pl.BlockSpec((1, tq, W), lambda i, b: (b, i, 0)),
                   pl.BlockSpec((1, tq, MEM_W), lambda i, b: (b, i, 0))],
        out_shape=[jax.ShapeDtypeStruct((B, S, W), BF16), jax.ShapeDtypeStruct((B, S, MEM_W), BF16)],
        compiler_params=_cparams(("arbitrary", "arbitrary")),
        name="local_attention_w%d" % window,
    )(*args)


def _gate(o_ref, g_ref):
    return (o_ref[...].astype(F32) * _silu(g_ref[...].astype(F32))).astype(BF16)


def _out_even_kernel(x_ref, o1, g1, o2, g2, o3, g3, w_ref, gpost_ref, y_ref):
    W = o1.shape[1]
    y = (_dot(_gate(o1, g1), w_ref[0:W, :]) + _dot(_gate(o2, g2), w_ref[W:2 * W, :])
         + _dot(_gate(o3, g3), w_ref[2 * W:3 * W, :]))
    y_ref[...] = x_ref[...] + _rms(y, gpost_ref[...])


def _out_even(x2, o1, g1, o2, g2, o3, g3, w, gpost):
    T, D = x2.shape
    tm = PROJ_TM
    row = lambda n: pl.BlockSpec((tm, n), lambda i: (i, 0))
    return pl.pallas_call(
        _out_even_kernel,
        grid=(T // tm,),
        in_specs=[row(D)] + [row(512)] * 6 + [_full(w.shape), _full((1, D))],
        out_specs=row(D),
        out_shape=jax.ShapeDtypeStruct((T, D), F32),
        compiler_params=_cparams(("arbitrary",)),
        name="out_proj_even",
    )(x2, o1, g1, o2, g2, o3, g3, w, gpost)


def _out_odd_kernel(x_ref, oc, os_, ow, gl_ref, gn, om, gm, ex_ref, w_ref, gpost_ref, y_ref):
    W = oc.shape[1]
    sig = jax.nn.sigmoid(gl_ref[...])
    hi = sig.astype(BF16)
    lo = (sig - hi.astype(F32)).astype(BF16)
    mix = jnp.zeros(oc.shape, F32)
    for br, o_ref in enumerate((oc, os_, ow)):
        g = _dot(hi, ex_ref[br]) + _dot(lo, ex_ref[br])
        mix = mix + g * o_ref[...].astype(F32)
    nsa = (mix * _silu(gn[...].astype(F32))).astype(BF16)
    y = _dot(nsa, w_ref[0:W, :]) + _dot(_gate(om, gm), w_ref[W:2 * W, :])
    y_ref[...] = x_ref[...] + _rms(y, gpost_ref[...])


def _out_odd(x2, oc, os_, ow, gl, gn, om, gm, ex, w, gpost):
    T, D = x2.shape
    tm = PROJ_TM
    row = lambda n: pl.BlockSpec((tm, n), lambda i: (i, 0))
    return pl.pallas_call(
        _out_odd_kernel,
        grid=(T // tm,),
        in_specs=[row(D), row(512), row(512), row(512), row(LANES), row(512), row(512), row(512),
                  _full(ex.shape), _full(w.shape), _full((1, D))],
        out_specs=row(D),
        out_shape=jax.ShapeDtypeStruct((T, D), F32),
        compiler_params=_cparams(("arbitrary",)),
        name="out_proj_odd",
    )(x2, oc, os_, ow, gl, gn, om, gm, ex, w, gpost)


O_Q, O_KC, O_VC, O_KS, O_KW, O_GL, O_GN, O_QM, O_GM, O_END = (
    0, 512, 640, 768, 896, 1024, 1152, 1664, 2176, 2688)


def _in_odd_kernel(x_ref, gpre_ref, w_ref, wvs_ref, wvw_ref, q_ref, kc_ref, vc_ref, ks_ref, vs_ref, kw_ref, vw_ref,
                   gl_ref, gn_ref, qm_ref, gm_ref):
    h = _rms(x_ref[...], gpre_ref[...]).astype(BF16)

    def proj(lo, hi):
        return _dot(h, w_ref[:, lo:hi])

    q_ref[...] = (proj(O_Q, O_KC) * (LOG2E * HEAD_DIM ** -0.5)).astype(BF16)
    kc_ref[...] = proj(O_KC, O_VC)
    vc_ref[...] = proj(O_VC, O_KS)
    ks_ref[...] = proj(O_KS, O_KW).astype(BF16)
    vs_ref[0] = _dot_nt(wvs_ref[...], h).astype(BF16)
    kw_ref[...] = proj(O_KW, O_GL).astype(BF16)
    _store_lane_tiles(vw_ref, _dot_nt(wvw_ref[...], h).astype(BF16))
    gl_ref[...] = proj(O_GL, O_GN)
    gn_ref[...] = proj(O_GN, O_QM).astype(BF16)
    qm_ref[...] = (proj(O_QM, O_GM) * (LOG2E * MEM_HEAD_DIM ** -0.5)).astype(BF16)
    gm_ref[...] = proj(O_GM, O_END).astype(BF16)


def _odd_weights(w_in):
    D = w_in.shape[0]
    c = lambda lo, hi: w_in[:, lo:hi]
    w = jnp.concatenate([
        _perm_heads(c(0, 512), 1),
        c(512, 640), c(640, 768), c(768, 896), c(1024, 1152),
        c(1280, 1304), jnp.zeros((D, LANES - 3 * N_HEADS), F32),
        _perm_heads(c(1304, 1816), 1),
        c(1816, 2328), c(2328, 2840)], axis=1).astype(BF16)
    return w, c(896, 1024).T.astype(BF16), c(1152, 1280).T.astype(BF16)


def _in_odd(x2, gpre, w, wvs_t, wvw_t):
    T, D = x2.shape
    tm = PROJ_TM
    row = lambda n: pl.BlockSpec((tm, n), lambda i: (i, 0))
    widths = [512, 128, 128, 128, 128, 128, 128, 128, 512, 512, 512]
    dtypes = [BF16, F32, F32, BF16, BF16, BF16, BF16, F32, BF16, BF16, BF16]
    out_specs = [row(n) for n in widths]
    out_shape = [jax.ShapeDtypeStruct((T, n), dt) for n, dt in zip(widths, dtypes)]
    out_specs[4] = pl.BlockSpec((1, LANES, tm), lambda i: (i, 0, 0))
    out_shape[4] = jax.ShapeDtypeStruct((T // tm, LANES, tm), BF16)
    out_specs[6] = pl.BlockSpec((tm // LANES, LANES, LANES), lambda i: (i, 0, 0))
    out_shape[6] = jax.ShapeDtypeStruct((T // LANES, LANES, LANES), BF16)
    return pl.pallas_call(
        _in_odd_kernel,
        grid=(T // tm,),
        in_specs=[row(D), _full((1, D)), _full(w.shape), _full(wvs_t.shape), _full(wvw_t.shape)],
        out_specs=out_specs,
        out_shape=out_shape,
        compiler_params=_cparams(("arbitrary",)),
        name="in_proj_odd",
    )(x2, gpre, w, wvs_t, wvw_t)


def _compress_kernel(rk_ref, rv_ref, pek_ref, pev_ref, w1k_ref, w1v_ref, w2k_ref, w2vt_ref, kc_ref, vct_ref, *, nc):
    def hidden(r_ref, pe_ref, w1_ref):
        r = r_ref[0]
        za = _dot((r + pe_ref[0:1, :]).astype(BF16), w1_ref[0])
        zb = _dot((r + pe_ref[1:2, :]).astype(BF16), w1_ref[1])
        return _silu(za + pltpu.roll(zb, r.shape[0] - 1, 0)).astype(BF16)

    kc = _dot(hidden(rk_ref, pek_ref, w1k_ref), w2k_ref[...])
    kc_ref[0] = jnp.where(lax.broadcasted_iota(jnp.int32, kc.shape, 0) < nc, kc, 0.0).astype(BF16)
    vct = _dot_nt(w2vt_ref[...], hidden(rv_ref, pev_ref, w1v_ref))
    vct_ref[0] = jnp.where(lax.broadcasted_iota(jnp.int32, vct.shape, 1) < nc, vct, 0.0).astype(BF16)


def _compress_weights(pe, w1, w2):
    eye = jnp.eye(N_KV, dtype=F32)
    d = HEAD_DIM
    w1h = w1.reshape(2, CMP_STRIDE, d, CMP_HIDDEN)
    w1x = jnp.einsum('ztdn,gh->ztgdhn', w1h, eye).reshape(2, CMP_STRIDE * N_KV * d, N_KV * CMP_HIDDEN)
    w2x = jnp.einsum('kn,gh->gkhn', w2, eye).reshape(N_KV * CMP_HIDDEN, N_KV * d)
    pex = jnp.broadcast_to(pe.reshape(2, CMP_STRIDE, 1, d), (2, CMP_STRIDE, N_KV, d)).reshape(2, -1)
    return pex, w1x.astype(BF16), w2x.astype(BF16)


def _compress(rk, rv, pek, pev, w1k, w1v, w2k, w2vt):
    B, n, W = rk.shape
    spec3 = lambda a: pl.BlockSpec(a.shape, lambda b: (0,) * a.ndim)
    rspec = pl.BlockSpec((1, n, W), lambda b: (b, 0, 0))
    return pl.pallas_call(
        functools.partial(_compress_kernel, nc=n - 1),
        grid=(B,),
        in_specs=[rspec, rspec, spec3(pek), spec3(pev), spec3(w1k), spec3(w1v), spec3(w2k), spec3(w2vt)],
        out_specs=[pl.BlockSpec((1, n, LANES), lambda b: (b, 0, 0)), pl.BlockSpec((1, LANES, n), lambda b: (b, 0, 0))],
        out_shape=[jax.ShapeDtypeStruct((B, n, LANES), BF16), jax.ShapeDtypeStruct((B, LANES, n), BF16)],
        compiler_params=_cparams(("arbitrary",)),
        name="compress_blocks",
    )(rk, rv, pek, pev, w1k, w1v, w2k, w2vt)


def _cmp_kernel(q_ref, kc_ref, vct_ref, bias_ref, mt_ref, o_ref, sel_ref, *, tq, ns, top_n):
    i = pl.program_id(0)
    n = kc_ref.shape[1]
    s = _dot_nt(kc_ref[0], _split_groups(q_ref, tq)) + bias_ref[0]
    any_visible = (lax.broadcasted_iota(jnp.int32, (1, tq), 1) + i * tq) >= CMP_LEN - 1

    out_t, psums = [], []
    for t in range(N_KV):
        ps = []
        psum = jnp.zeros((n, tq), F32)
        for p in range(GROUP):
            h = t * GROUP + p
            sp = s[:, h * tq:(h + 1) * tq]
            pr = jnp.exp2(sp - jnp.max(sp, axis=0, keepdims=True))
            den = jnp.sum(pr, axis=0, keepdims=True)
            pr = pr * jnp.where(any_visible, 1.0 / den, 0.0)
            psum = psum + pr
            ps.append(pr.astype(BF16))
        acc = _dot(vct_ref[0, t * HEAD_DIM:(t + 1) * HEAD_DIM, :], jnp.concatenate(ps, axis=1))
        out_t += [acc[:, p * tq:(p + 1) * tq] for p in range(GROUP)]
        psums.append(psum)
    o_ref[0] = _merge_heads(out_t).astype(BF16)

    blk = lax.broadcasted_iota(jnp.int32, (LANES, tq), 0)
    q_idx = lax.broadcasted_iota(jnp.int32, (LANES, tq), 1) + i * tq
    cur = q_idx // SEL_LEN
    forced = (blk == 0) | (blk == cur) | (blk == cur - 1)
    causal = blk * SEL_LEN <= q_idx
    for t in range(N_KV):
        hi = psums[t].astype(BF16)
        lo = (psums[t] - hi.astype(F32)).astype(BF16)
        imp = _dot(mt_ref[...], hi) + _dot(mt_ref[...], lo)
        score = jnp.where(forced, SEL_FORCE, jnp.where(causal, imp, -SEL_FORCE))
        score = jnp.where(blk < ns, score, LOWEST)
        sel = jnp.zeros((LANES, tq), F32)
        for _ in range(top_n):
            mx = jnp.max(score, axis=0, keepdims=True)
            idx = jnp.min(jnp.where(score == mx, blk, LANES), axis=0, keepdims=True)
            pick = blk == idx
            sel = jnp.where(pick, 1.0, sel)
            score = jnp.where(pick, LOWEST, score)
        sel_ref[0, t] = jnp.where(causal, sel, 0.0).T.astype(BF16)


def _cmp_to_sel_t(n, ns):
    c_start = np.arange(n)[None, :] * CMP_STRIDE
    s_start = np.arange(LANES)[:, None] * SEL_LEN
    overlap = np.maximum(np.minimum(c_start + CMP_LEN, s_start + SEL_LEN) - np.maximum(c_start, s_start), 0)
    m = overlap.astype(np.float32) / CMP_LEN
    m[:, n - 1:] = 0.0
    m[ns:, :] = 0.0
    return jnp.asarray(m, BF16)


def _cmp_select(q, kc, vct, bias, mt):
    B, S, W = q.shape
    n = kc.shape[1]
    tq = bias.shape[2] // N_HEADS
    ns = S // SEL_LEN
    return pl.pallas_call(
        functools.partial(_cmp_kernel, tq=tq, ns=ns, top_n=min(SEL_TOP_N, ns)),
        grid=(S // tq, B),
        in_specs=[pl.BlockSpec((1, tq, W), lambda i, b: (b, i, 0)),
                  pl.BlockSpec((1, n, LANES), lambda i, b: (b, 0, 0)),
                  pl.BlockSpec((1, LANES, n), lambda i, b: (b, 0, 0)),
                  pl.BlockSpec((1, n, N_HEADS * tq), lambda i, b: (i, 0, 0)),
                  pl.BlockSpec(mt.shape, lambda i, b: (0, 0))],
        out_specs=[pl.BlockSpec((1, tq, W), lambda i, b: (b, i, 0)),
                   pl.BlockSpec((1, N_KV, tq, LANES), lambda i, b: (b, 0, i, 0))],
        out_shape=[jax.ShapeDtypeStruct((B, S, W), BF16),
                   jax.ShapeDtypeStruct((B, N_KV, S, LANES), BF16)],
        compiler_params=_cparams(("arbitrary", "arbitrary")),
        name="cmp_select",
    )(q, kc, vct, bias, mt)


def _slc_kernel(qmin_ref, kmax_ref, qmin_sub_ref, kmax_sub_ref, tab_ref, q_ref, sel_ref, pq_ref, pk_ref, k_ref,
                vt_ref, o_ref,
                qs_scr, s_scr, mx_scr, m_scr, acc_scr, *, tq, tk):
    i = pl.program_id(1)
    m_scr[...] = jnp.full(m_scr.shape, NEG_INF, F32)
    acc_scr[...] = jnp.zeros(acc_scr.shape, F32)
    qs_scr[:, :LANES] = _split_groups(q_ref, tq)
    for t in range(N_KV):
        off = jnp.where(sel_ref[0, t].astype(F32) > 0.5, 0.0, NEG_INF).astype(BF16)
        for p in range(GROUP):
            h = t * GROUP + p
            qs_scr[h * tq:(h + 1) * tq, LANES:] = off
    ones = jnp.ones((BF16_ROWS, tk), BF16)
    gq = GROUP * tq
    last = (i * tq + tq - 1) // tk

    def scores(j, slot):
        start = pl.multiple_of(j * tk, tk)
        blk = lax.broadcasted_iota(jnp.int32, (tk, LANES), 0) // SEL_LEN + j * (tk // SEL_LEN)
        onehot = jnp.where(lax.broadcasted_iota(jnp.int32, (tk, LANES), 1) == blk, 1.0, 0.0).astype(BF16)
        kx = jnp.concatenate([k_ref[0, pl.ds(start, tk), :], onehot], axis=1)
        for t in range(N_KV):
            s = _dot_nt(kx, qs_scr[t * gq:(t + 1) * gq, :])
            s_scr[slot, t] = s
            mx_scr[slot, t] = jnp.max(s, axis=0, keepdims=True)

    def add_bias(j, slot):
        for kb in range(tk // LANES):
            for qb in range(tq // LANES):
                near_sub = (qmin_sub_ref[i * (tq // LANES) + qb] - kmax_sub_ref[j * (tk // LANES) + kb]) < FAR_DIST

                @pl.when(near_sub)
                def _(kb=kb, qb=qb):
                    rows = slice(kb * LANES, (kb + 1) * LANES)
                    dist = pq_ref[0, qb * LANES:(qb + 1) * LANES, :] - pk_ref[j, :, rows]
                    bucket = _t5_bucket(dist.astype(F32).T)
                    for h in range(N_HEADS):
                        t, p = divmod(h, GROUP)
                        cols = slice(p * tq + qb * LANES, p * tq + (qb + 1) * LANES)
                        s_scr[slot, t, rows, cols] = s_scr[slot, t, rows, cols] + _bias_lookup(bucket, tab_ref, h)

    def accumulate(j, slot, near, diag):
        if diag:
            k_idx = lax.broadcasted_iota(jnp.int32, (tk, tq), 0) + j * tk
            q_idx = lax.broadcasted_iota(jnp.int32, (tk, tq), 1) + i * tq
            tok_ok = k_idx <= q_idx
        for t in range(N_KV):
            ps, alphas = [], []
            for p in range(GROUP):
                cols = slice(p * tq, (p + 1) * tq)
                sp = s_scr[slot, t, :, cols]
                if diag:
                    sp = jnp.where(tok_ok, sp, NEG_INF)
                tile_max = jnp.max(sp, axis=0, keepdims=True) if near or diag else mx_scr[slot, t, :, cols]
                m_old = m_scr[t, :, cols]
                m_new = jnp.maximum(m_old, tile_max)
                alphas.append(jnp.exp2(m_old - m_new))
                ps.append(jnp.exp2(sp - m_new).astype(BF16))
                m_scr[t, :, cols] = m_new
            vt = jnp.concatenate([vt_ref[j, t * HEAD_DIM:(t + 1) * HEAD_DIM, :], ones], axis=0)
            acc_scr[t] = (jnp.concatenate(alphas, axis=1) * acc_scr[t]
                          + _dot(vt, jnp.concatenate(ps, axis=1)))

    def step(j_next, slot_next, j, slot):
        far = (qmin_ref[i] - kmax_ref[j]) >= FAR_DIST

        @pl.when(far)
        def _():
            scores(j_next, slot_next)
            accumulate(j, slot, False, False)

        @pl.when(jnp.logical_not(far))
        def _():
            add_bias(j, slot)

        @pl.when(jnp.logical_not(far))
        def _():
            scores(j_next, slot_next)
            accumulate(j, slot, True, False)

    def pair(jj, carry):
        step(2 * jj + 1, 1, 2 * jj, 0)
        step(2 * jj + 2, 0, 2 * jj + 1, 1)
        return carry

    scores(0, 0)
    lax.fori_loop(0, last // 2, pair, 0)

    @pl.when(last % 2 == 1)
    def _():
        step(last, 1, last - 1, 0)

    for slot in range(2):
        @pl.when(last % 2 == slot)
        def _(slot=slot):
            add_bias(last, slot)

        @pl.when(last % 2 == slot)
        def _(slot=slot):
            accumulate(last, slot, True, True)

    out_t = []
    for t in range(N_KV):
        for p in range(GROUP):
            cols = slice(p * tq, (p + 1) * tq)
            out_t.append(acc_scr[t, :HEAD_DIM, cols] / acc_scr[t, HEAD_DIM:HEAD_DIM + 1, cols])
    o_ref[0] = _merge_heads(out_t).astype(BF16)


def _slc_attention(q, sel, k, vt, pos, tab_rel):
    B, S, W = q.shape
    tk = vt.shape[-1]
    tq = min(SLC_TQ, S)
    nq, nk = S // tq, S // tk
    qmin_sub = pos.reshape(S // LANES, LANES).min(axis=1)
    kmax_sub = pos.reshape(S // LANES, LANES).max(axis=1)
    qmin = qmin_sub.reshape(nq, tq // LANES).min(axis=1)
    kmax = kmax_sub.reshape(nk, tk // LANES).max(axis=1)
    smem = pl.BlockSpec(memory_space=pltpu.SMEM)
    return pl.pallas_call(
        functools.partial(_slc_kernel, tq=tq, tk=tk),
        grid=(B, nq),
        in_specs=[smem, smem, smem, smem,
                  pl.BlockSpec(tab_rel.shape, lambda b, i: (0, 0)),
                  pl.BlockSpec((1, tq, W), lambda b, i: (b, i, 0)),
                  pl.BlockSpec((1, N_KV, tq, LANES), lambda b, i: (b, 0, i, 0)),
                  pl.BlockSpec((1, tq, 1), lambda b, i: (i, 0, 0)),
                  pl.BlockSpec((nk, 1, tk), lambda b, i: (0, 0, 0)),
                  pl.BlockSpec((1, S, LANES), lambda b, i: (b, 0, 0)),
                  pl.BlockSpec((nk, LANES, tk), lambda b, i: (b, 0, 0))],
        out_specs=pl.BlockSpec((1, tq, W), lambda b, i: (b, i, 0)),
        out_shape=jax.ShapeDtypeStruct((B, S, W), BF16),
        scratch_shapes=[pltpu.VMEM((N_HEADS * tq, 2 * LANES), BF16),
                        pltpu.VMEM((2, N_KV, tk, GROUP * tq), F32),
                        pltpu.VMEM((2, N_KV, 1, GROUP * tq), F32),
                        pltpu.VMEM((N_KV, 1, GROUP * tq), F32),
                        pltpu.VMEM((N_KV, ACC_ROWS, GROUP * tq), F32)],
        compiler_params=_cparams(("arbitrary", "arbitrary")),
        name="slc_attention",
    )(qmin, kmax, qmin_sub, kmax_sub, tab_rel, q, sel, pos.reshape(nq, tq, 1), pos.reshape(nk, 1, tk), k, vt)


def _band_positions(pos, tq, pad):
    S = pos.shape[0]
    nq = S // tq
    rows = jnp.pad(pos, (pad, 0)).reshape(nq + pad // tq, tq)
    win = jnp.concatenate([rows[n:n + nq] for n in range(pad // tq + 1)], axis=1)
    return pos.reshape(nq, 1, tq), win.reshape(nq, pad + tq, 1)


def _bias_table(table):
    t = jnp.zeros((N_HEADS, LANES), F32)
    return t.at[:, :NUM_BUCKETS].set(table.astype(F32).T * LOG2E)


def _gate_expand():
    ex = np.zeros((3, LANES, N_HEADS * HEAD_DIM), np.float32)
    for slot, h in enumerate(HEAD_ORDER):
        for br in range(3):
            ex[br, 3 * h + br, slot * HEAD_DIM:(slot + 1) * HEAD_DIM] = 1.0
    return jnp.asarray(ex, BF16)


def kernel(x, mem, positions, rel_bias_table, norm_pre, norm_post, mem_norm, w_mem_kv, w_in_even, mla_q_norm,
           mla_kv_norm, mla_w_uq, mla_w_ukv, swa_sinks, w_out_even, w_in_odd, cmp_pos_k, cmp_pos_v, cmp_w1_k,
           cmp_w2_k, cmp_w1_v, cmp_w2_v, w_out_odd):
    B, S, D = x.shape
    depth = norm_pre.shape[0]
    T = B * S
    pos = positions.astype(jnp.int32)
    n_cmp = S // CMP_STRIDE

    tab = _bias_table(rel_bias_table)
    tab_rel = _bias_table(rel_bias_table - rel_bias_table[NUM_BUCKETS - 1:, :])
    pq, pk_swa = _band_positions(pos, BAND_TQ, BAND_TQ)
    bias_swa = _bias_tiles(tab, pq, pk_swa, SWA_WINDOW)
    _, pk_win = _band_positions(pos, BAND_TQ, (-(-(NSA_WINDOW - 1) // BAND_TQ)) * BAND_TQ)
    bias_win = _bias_tiles(tab, pq, pk_win, NSA_WINDOW)
    pos_cmp = jnp.concatenate([pos[CMP_LEN - 1::CMP_STRIDE], pos[-1:]])
    cmp_tq = min(CMP_TQ, S)
    bias_cmp = _bias_tiles(tab, pos.reshape(S // cmp_tq, 1, cmp_tq), pos_cmp.reshape(1, n_cmp, 1), None)

    ctab, stab = _rope_tables(pos)
    mem_k, mem_vt = _mem_kv(mem, mem_norm, w_mem_kv)
    m_sel_t = _cmp_to_sel_t(n_cmp, S // SEL_LEN)
    gate_ex = _gate_expand()

    x2 = x.reshape(T, D)
    r3 = lambda a: a.reshape(B, S, a.shape[-1])
    for l in range(depth):
        gpre = norm_pre[l].reshape(1, D)
        gpost = norm_post[l].reshape(1, D)
        if l % 2 == 0:
            e = l // 2
            w, wvs_t, wq, wqs, wk, wv = _even_weights(w_in_even[e], mla_w_uq[e], mla_w_ukv[e])
            (qmla, kmla, vmla_t, gmla, qswa, kswa, vswa_t, gswa, qmem, gmem) = _in_even(
                x2, gpre, w, wvs_t, mla_q_norm[e].reshape(1, -1), mla_kv_norm[e].reshape(1, -1),
                wq, wqs, wk, wv, ctab, stab)
            o_mla = _mla_attention(r3(qmla), r3(kmla), vmla_t)
            o_swa, o_mem = _local_attention(r3(qswa), r3(kswa), vswa_t, bias_swa, r3(qmem),
                                            mem_k[l], mem_vt[l], SWA_WINDOW, sinks=swa_sinks[e] * LOG2E)
            w_o = w_out_even[e]
            w_o = jnp.concatenate([w_o[:512], _perm_heads(w_o[512:1024], 0), w_o[1024:]], axis=0).astype(BF16)
            x2 = _out_even(x2, o_mla.reshape(T, -1), gmla, o_swa.reshape(T, -1), gswa,
                           o_mem.reshape(T, -1), gmem, w_o, gpost)
        else:
            o = l // 2
            w, wvs_t, wvw_t = _odd_weights(w_in_odd[o])
            (q, kc_in, vc_in, ks, vs_t, kw, vw_t, gl, gn, qmem, gmem) = _in_odd(x2, gpre, w, wvs_t, wvw_t)
            pek, w1k, w2k = _compress_weights(cmp_pos_k[o], cmp_w1_k[o], cmp_w2_k[o])
            pev, w1v, w2v = _compress_weights(cmp_pos_v[o], cmp_w1_v[o], cmp_w2_v[o])
            chunks = lambda a: a.reshape(B, n_cmp, CMP_STRIDE * LANES)
            kc, vc_t = _compress(chunks(kc_in), chunks(vc_in), pek, pev, w1k, w1v, w2k, w2v.T)
            o_cmp, sel = _cmp_select(r3(q), kc, vc_t, bias_cmp, m_sel_t)
            o_slc = _slc_attention(r3(q), sel, r3(ks), vs_t, pos, tab_rel)
            o_win, o_mem = _local_attention(r3(q), r3(kw), vw_t, bias_win, r3(qmem),
                                            mem_k[l], mem_vt[l], NSA_WINDOW)
            w_o = w_out_odd[o]
            w_o = jnp.concatenate([_perm_heads(w_o[:512], 0), w_o[512:]], axis=0).astype(BF16)
            x2 = _out_odd(x2, o_cmp.reshape(T, -1), o_slc.reshape(T, -1), o_win.reshape(T, -1), gl, gn,
                          o_mem.reshape(T, -1), gmem, gate_ex, w_o, gpost)
    return x2.reshape(B, S, D)
```

```python
import functools
import math

import numpy as np
import jax
import jax.numpy as jnp
from jax import lax
from jax.experimental import pallas as pl
from jax.experimental.pallas import tpu as pltpu

F32 = jnp.float32
BF16 = jnp.bfloat16

NEG_INF = -1e30
EPS = 1e-6
SEL_FORCE = 1e9
LOWEST = -3.0e38
LOG2E = math.log2(math.e)

NUM_BUCKETS = 32
REL_MAX_DIST = 128

MLA_HEADS = 8
MLA_Q_RANK = 256
MLA_KV_RANK = 128
MLA_NOPE = 64
MLA_ROPE = 32
MLA_V = 64
ROPE_BASE = 10000.0
ROPE_HALF = MLA_ROPE // 2

HEAD_DIM = 64
N_HEADS = 8
N_KV = 2
GROUP = N_HEADS // N_KV
SWA_WINDOW = 128
NSA_WINDOW = 512
CMP_LEN = 32
CMP_STRIDE = 16
CMP_HIDDEN = 256
SEL_LEN = 64
SEL_TOP_N = 16
MEM_HEADS = 4
MEM_HEAD_DIM = 128
MEM_W = MEM_HEADS * MEM_HEAD_DIM

LANES = 128
BF16_ROWS = 16
ACC_ROWS = HEAD_DIM + BF16_ROWS
BAND_TQ = 128
CMP_TQ = 256
MLA_TQ = 512
MLA_TK = 512
MLA_HEADS_PER_STEP = 2
SLC_TQ = 256
SLC_TK = 512
MEM_TQ = 1024
PROJ_TM = 512
VMEM_LIMIT = 56 * 1024 * 1024


def _t5_far_distance():
    max_exact = NUM_BUCKETS // 2
    d = np.arange(0, 2 * REL_MAX_DIST)
    x = (np.log(np.maximum(d, 1).astype(np.float64) / max_exact)
         / math.log(REL_MAX_DIST / max_exact) * (NUM_BUCKETS - max_exact))
    assert np.abs(x - np.round(x))[max_exact + 1:REL_MAX_DIST].min() > 1e-3
    bucket = np.where(d < max_exact, d, np.minimum(max_exact + np.trunc(x).astype(np.int64), NUM_BUCKETS - 1))
    assert (np.diff(bucket) >= 0).all()
    return int(np.argmax(bucket >= NUM_BUCKETS - 1))


FAR_DIST = _t5_far_distance()

HEAD_ORDER = [h for p in range(GROUP) for h in (p, GROUP + p)]


def _cparams(sem):
    return pltpu.CompilerParams(dimension_semantics=sem, vmem_limit_bytes=VMEM_LIMIT)


def _dot(a, b):
    return jnp.dot(a, b, preferred_element_type=F32)


def _dot_nt(a, b):
    return lax.dot_general(a, b, (((1,), (1,)), ((), ())), preferred_element_type=F32)


def _rms(x, g):
    return x * lax.rsqrt(jnp.mean(x * x, axis=-1, keepdims=True) + EPS) * g


def _silu(x):
    return x * jax.nn.sigmoid(x)


def _t5_bucket(dist):
    max_exact = NUM_BUCKETS // 2
    d = jnp.maximum(dist, 0.0)
    scale = (NUM_BUCKETS - max_exact) / math.log2(REL_MAX_DIST / max_exact)
    large = max_exact + jnp.floor(jnp.log2(jnp.maximum(d, 1.0) * (1.0 / max_exact)) * scale)
    return jnp.where(d < max_exact, d, jnp.minimum(large, NUM_BUCKETS - 1.0)).astype(jnp.int32)


def _bias_lookup(bucket, tab_ref, head):
    rows = jnp.broadcast_to(tab_ref[head:head + 1, :], (bucket.shape[0], LANES))
    return jnp.concatenate(
        [jnp.take_along_axis(rows, bucket[:, c * LANES:(c + 1) * LANES], axis=1)
         for c in range(bucket.shape[1] // LANES)], axis=1)


def _split_groups(q_ref, tq, b=0):
    first = lax.broadcasted_iota(jnp.int32, (tq, LANES), 1) < HEAD_DIM
    parts = []
    for t in range(N_KV):
        for p in range(GROUP):
            qp = q_ref[b, :, p * LANES:(p + 1) * LANES]
            parts.append(jnp.where(first if t == 0 else ~first, qp, jnp.zeros_like(qp)))
    return jnp.concatenate(parts, axis=0)


def _merge_heads(out_t):
    blocks = [jnp.concatenate([out_t[p], out_t[GROUP + p]], axis=0).T for p in range(GROUP)]
    return jnp.concatenate(blocks, axis=1)


def _band_visible(i, kl, tq, window):
    pad = kl - tq
    r = lax.broadcasted_iota(jnp.int32, (kl, tq), 0)
    c = lax.broadcasted_iota(jnp.int32, (kl, tq), 1)
    delta = c + pad - r
    return ((r + (i * tq - pad)) >= 0) & (delta >= 0) & (delta < window)


def _cmp_visible(i, n, tq):
    r = lax.broadcasted_iota(jnp.int32, (n, tq), 0)
    c = lax.broadcasted_iota(jnp.int32, (n, tq), 1)
    return (r * CMP_STRIDE + (CMP_LEN - 1)) <= (c + i * tq)


def _bias_kernel(tab_ref, pq_ref, pk_ref, o_ref, *, tq, window):
    i = pl.program_id(0)
    kl = pk_ref.shape[1]
    bucket = _t5_bucket((pq_ref[0] - pk_ref[0]).astype(F32))
    visible = _cmp_visible(i, kl, tq) if window is None else _band_visible(i, kl, tq, window)
    mask_add = jnp.where(visible, 0.0, NEG_INF)
    for h in range(N_HEADS):
        o_ref[0, :, h * tq:(h + 1) * tq] = _bias_lookup(bucket, tab_ref, h) + mask_add


def _bias_tiles(tab, pos_q, pos_k, window):
    nq, _, tq = pos_q.shape
    kl = pos_k.shape[1]
    per_tile_keys = window is not None
    return pl.pallas_call(
        functools.partial(_bias_kernel, tq=tq, window=window),
        grid=(nq,),
        in_specs=[pl.BlockSpec(tab.shape, lambda i: (0, 0)),
                  pl.BlockSpec((1, 1, tq), lambda i: (i, 0, 0)),
                  pl.BlockSpec((1, kl, 1), (lambda i: (i, 0, 0)) if per_tile_keys else (lambda i: (0, 0, 0)))],
        out_specs=pl.BlockSpec((1, kl, N_HEADS * tq), lambda i: (i, 0, 0)),
        out_shape=jax.ShapeDtypeStruct((nq, kl, N_HEADS * tq), F32),
        compiler_params=_cparams(("arbitrary",)),
        name="bias_tiles",
    )(tab, pos_q, pos_k)


def _memkv_kernel(mem_ref, g_ref, wk_ref, wvt_ref, k_ref, vt_ref):
    mn = _rms(mem_ref[0], g_ref[0]).astype(BF16)
    k_ref[0, 0] = _dot(mn, wk_ref[0]).astype(BF16)
    vt_ref[0, 0] = _dot_nt(wvt_ref[0], mn).astype(BF16)


def _mem_kv(mem, mem_norm, w_mem_kv):
    B, M, D = mem.shape
    L = mem_norm.shape[0]
    wk = w_mem_kv[:, :, :MEM_W].astype(BF16)
    wvt = jnp.swapaxes(w_mem_kv[:, :, MEM_W:], 1, 2).astype(BF16)
    return pl.pallas_call(
        _memkv_kernel,
        grid=(L, B),
        in_specs=[pl.BlockSpec((1, M, D), lambda l, b: (b, 0, 0)),
                  pl.BlockSpec((1, 1, D), lambda l, b: (l, 0, 0)),
                  pl.BlockSpec((1, D, MEM_W), lambda l, b: (l, 0, 0)),
                  pl.BlockSpec((1, MEM_W, D), lambda l, b: (l, 0, 0))],
        out_specs=[pl.BlockSpec((1, 1, M, MEM_W), lambda l, b: (l, b, 0, 0)),
                   pl.BlockSpec((1, 1, MEM_W, M), lambda l, b: (l, b, 0, 0))],
        out_shape=[jax.ShapeDtypeStruct((L, B, M, MEM_W), BF16), jax.ShapeDtypeStruct((L, B, MEM_W, M), BF16)],
        compiler_params=_cparams(("arbitrary", "arbitrary")),
        name="mem_kv",
    )(mem, mem_norm.reshape(L, 1, D), wk, wvt)


E_CQ, E_CKV, E_KR, E_KRS, E_GMLA, E_QSWA, E_KSWA, E_GSWA, E_QMEM, E_GMEM, E_END = (
    0, 256, 384, 512, 640, 1152, 1664, 1792, 2304, 2816, 3328)


def _store_lane_tiles(ref, val):
    for c in range(val.shape[1] // LANES):
        ref[c] = val[:, c * LANES:(c + 1) * LANES]


def _in_even_kernel(x_ref, gpre_ref, w_ref, wvs_ref, qn_ref, kvn_ref, wq_ref, wqs_ref, wk_ref, wv_ref, c_ref, s_ref,
                    qmla_ref, kmla_ref, vmla_ref, gmla_ref, qswa_ref, kswa_ref, vswa_ref, gswa_ref,
                    qmem_ref, gmem_ref):
    h = _rms(x_ref[...], gpre_ref[...]).astype(BF16)

    def proj(lo, hi):
        return _dot(h, w_ref[:, lo:hi])

    cos = c_ref[...]
    sin = s_ref[...]
    cos8 = jnp.concatenate([cos] * MLA_HEADS, axis=1)
    sin8 = jnp.concatenate([sin] * MLA_HEADS, axis=1)

    cq = _rms(proj(E_CQ, E_CKV), qn_ref[...]).astype(BF16)
    q = _dot(cq, wq_ref[...]) * cos8 + _dot(cq, wqs_ref[...]) * sin8
    qmla_ref[...] = (q * (LOG2E * (MLA_NOPE + MLA_ROPE) ** -0.5)).astype(BF16)

    ckv = _rms(proj(E_CKV, E_KR), kvn_ref[...]).astype(BF16)
    kpe = proj(E_KR, E_KRS) * cos + proj(E_KRS, E_GMLA) * sin
    kmla_ref[...] = (_dot(ckv, wk_ref[...]) + jnp.concatenate([kpe] * MLA_HEADS, axis=1)).astype(BF16)
    vmla_ref[0] = _dot_nt(wv_ref[...], ckv).astype(BF16)

    gmla_ref[...] = proj(E_GMLA, E_QSWA).astype(BF16)
    qswa_ref[...] = (proj(E_QSWA, E_KSWA) * (LOG2E * HEAD_DIM ** -0.5)).astype(BF16)
    kswa_ref[...] = proj(E_KSWA, E_GSWA).astype(BF16)
    _store_lane_tiles(vswa_ref, _dot_nt(wvs_ref[...], h).astype(BF16))
    gswa_ref[...] = proj(E_GSWA, E_QMEM).astype(BF16)
    qmem_ref[...] = (proj(E_QMEM, E_GMEM) * (LOG2E * MEM_HEAD_DIM ** -0.5)).astype(BF16)
    gmem_ref[...] = proj(E_GMEM, E_END).astype(BF16)


def _perm_heads(w, axis):
    w = jnp.moveaxis(w, axis, 0)
    rest = w.shape[1:]
    w = w.reshape((N_KV, GROUP, HEAD_DIM) + rest).swapaxes(0, 1).reshape((N_HEADS * HEAD_DIM,) + rest)
    return jnp.moveaxis(w, 0, axis)


def _even_weights(w_in, w_uq, w_ukv):
    D = w_in.shape[0]
    z = lambda n: jnp.zeros((D, n), F32)
    c = lambda lo, hi: w_in[:, lo:hi]
    w = jnp.concatenate([
        c(0, 256), c(256, 384),
        z(MLA_NOPE), c(384, 416), z(32),
        z(MLA_NOPE), c(400, 416), c(384, 400), z(32),
        c(416, 928),
        _perm_heads(c(928, 1440), 1), c(1440, 1568),
        _perm_heads(c(1696, 2208), 1),
        c(2208, 2720), c(2720, 3232)], axis=1).astype(BF16)
    wvs_t = c(1568, 1696).T.astype(BF16)
    hq = MLA_NOPE + MLA_ROPE
    zq = lambda n: jnp.zeros((MLA_Q_RANK, n), F32)
    wq = jnp.concatenate([jnp.concatenate([w_uq[:, h * hq:(h + 1) * hq], zq(32)], axis=1)
                          for h in range(MLA_HEADS)], axis=1).astype(BF16)
    wqs = jnp.concatenate([jnp.concatenate([zq(MLA_NOPE),
                                            w_uq[:, h * hq + MLA_NOPE + ROPE_HALF:(h + 1) * hq],
                                            w_uq[:, h * hq + MLA_NOPE:h * hq + MLA_NOPE + ROPE_HALF],
                                            zq(32)], axis=1)
                           for h in range(MLA_HEADS)], axis=1).astype(BF16)
    hk = MLA_NOPE + MLA_V
    zk = jnp.zeros((MLA_KV_RANK, LANES - MLA_NOPE), F32)
    wk = jnp.concatenate([jnp.concatenate([w_ukv[:, h * hk:h * hk + MLA_NOPE], zk], axis=1)
                          for h in range(MLA_HEADS)], axis=1).astype(BF16)
    wv_t = jnp.concatenate([w_ukv[:, h * hk + MLA_NOPE:(h + 1) * hk] for h in range(MLA_HEADS)],
                           axis=1).T.astype(BF16)
    return w, wvs_t, wq, wqs, wk, wv_t


def _rope_tables(pos):
    inv = ROPE_BASE ** (-jnp.arange(ROPE_HALF, dtype=F32) / ROPE_HALF)
    ang = pos.astype(F32)[:, None] * inv[None, :]
    cos, sin = jnp.cos(ang), jnp.sin(ang)
    S = pos.shape[0]
    ones, z32 = jnp.ones((S, MLA_NOPE), F32), jnp.zeros((S, 32), F32)
    ctab = jnp.concatenate([ones, cos, cos, z32], axis=1)
    stab = jnp.concatenate([jnp.zeros((S, MLA_NOPE), F32), -sin, sin, z32], axis=1)
    return ctab, stab


def _full(shape):
    return pl.BlockSpec(shape, lambda i: (0,) * len(shape))


def _in_even(x2, gpre, w, wvs_t, qn, kvn, wq, wqs, wk, wv, ctab, stab):
    T, D = x2.shape
    S = ctab.shape[0]
    tm = PROJ_TM
    ns = S // tm
    row = lambda n: pl.BlockSpec((tm, n), lambda i: (i, 0))
    widths = [1024, 1024, 512, 512, 512, 128, 128, 512, 512, 512]
    out_specs = [row(n) for n in widths]
    out_shape = [jax.ShapeDtypeStruct((T, n), BF16) for n in widths]
    out_specs[2] = pl.BlockSpec((1, widths[2], tm), lambda i: (i, 0, 0))
    out_shape[2] = jax.ShapeDtypeStruct((T // tm, widths[2], tm), BF16)
    out_specs[6] = pl.BlockSpec((tm // LANES, LANES, LANES), lambda i: (i, 0, 0))
    out_shape[6] = jax.ShapeDtypeStruct((T // LANES, LANES, LANES), BF16)
    return pl.pallas_call(
        _in_even_kernel,
        grid=(T // tm,),
        in_specs=[row(D), _full((1, D)), _full(w.shape), _full(wvs_t.shape), _full((1, MLA_Q_RANK)),
                  _full((1, MLA_KV_RANK)), _full(wq.shape), _full(wqs.shape), _full(wk.shape), _full(wv.shape),
                  pl.BlockSpec((tm, LANES), lambda i: (i % ns, 0)),
                  pl.BlockSpec((tm, LANES), lambda i: (i % ns, 0))],
        out_specs=out_specs,
        out_shape=out_shape,
        compiler_params=_cparams(("arbitrary",)),
        name="in_proj_even",
    )(x2, gpre, w, wvs_t, qn, kvn, wq, wqs, wk, wv, ctab, stab)


def _mla_kernel(q_ref, k_ref, vt_ref, o_ref, s_scr, mx_scr, m_scr, acc_scr, *, tq, tk, nvt, hg):
    i = pl.program_id(2)
    m_scr[...] = jnp.full(m_scr.shape, NEG_INF, F32)
    acc_scr[...] = jnp.zeros(acc_scr.shape, F32)
    ones = jnp.ones((BF16_ROWS, tk), BF16)
    last = (i * tq + tq - 1) // tk

    def scores(j, slot):
        start = pl.multiple_of(j * tk, tk)
        for t in range(hg):
            s = _dot_nt(k_ref[0, pl.ds(start, tk), t * LANES:(t + 1) * LANES],
                        q_ref[0, :, t * LANES:(t + 1) * LANES])
            s_scr[slot, t] = s
            mx_scr[slot, t] = jnp.max(s, axis=0, keepdims=True)

    def accumulate(j, slot, diag):
        for t in range(hg):
            s = s_scr[slot, t]
            if diag:
                k_idx = lax.broadcasted_iota(jnp.int32, (tk, tq), 0) + j * tk
                q_idx = lax.broadcasted_iota(jnp.int32, (tk, tq), 1) + i * tq
                s = jnp.where(k_idx <= q_idx, s, NEG_INF)
                tile_max = jnp.max(s, axis=0, keepdims=True)
            else:
                tile_max = mx_scr[slot, t]
            m_old = m_scr[t]
            m_new = jnp.maximum(m_old, tile_max)
            alpha = jnp.exp2(m_old - m_new)
            p = jnp.exp2(s - m_new).astype(BF16)
            m_scr[t] = m_new
            vt = jnp.concatenate([vt_ref[j * nvt + n, t * MLA_V:(t + 1) * MLA_V, :] for n in range(nvt)], axis=1)
            vt = jnp.concatenate([vt, ones], axis=0)
            acc_scr[t] = alpha * acc_scr[t] + _dot(vt, p)

    def pair(jj, carry):
        scores(2 * jj + 1, 1)
        accumulate(2 * jj, 0, False)
        scores(2 * jj + 2, 0)
        accumulate(2 * jj + 1, 1, False)
        return carry

    scores(0, 0)
    lax.fori_loop(0, last // 2, pair, 0)

    @pl.when(last % 2 == 0)
    def _():
        accumulate(last, 0, True)

    @pl.when(last % 2 == 1)
    def _():
        scores(last, 1)
        accumulate(last - 1, 0, False)
        accumulate(last, 1, True)

    out_t = jnp.concatenate([acc_scr[t, :MLA_V] / acc_scr[t, MLA_V:MLA_V + 1] for t in range(hg)], axis=0)
    o_ref[0] = out_t.T.astype(BF16)


def _mla_attention(q, k, vt):
    B, S, _ = q.shape
    vtile = vt.shape[-1]
    tq = min(MLA_TQ, S)
    tk = min(MLA_TK, S)
    hg = MLA_HEADS_PER_STEP
    return pl.pallas_call(
        functools.partial(_mla_kernel, tq=tq, tk=tk, nvt=tk // vtile, hg=hg),
        grid=(B, MLA_HEADS // hg, S // tq),
        in_specs=[pl.BlockSpec((1, tq, hg * LANES), lambda b, p, i: (b, i, p)),
                  pl.BlockSpec((1, S, hg * LANES), lambda b, p, i: (b, 0, p)),
                  pl.BlockSpec((S // vtile, hg * MLA_V, vtile), lambda b, p, i: (b, p, 0))],
        out_specs=pl.BlockSpec((1, tq, hg * MLA_V), lambda b, p, i: (b, i, p)),
        out_shape=jax.ShapeDtypeStruct((B, S, MLA_HEADS * MLA_V), BF16),
        scratch_shapes=[pltpu.VMEM((2, hg, tk, tq), F32), pltpu.VMEM((2, hg, 1, tq), F32),
                        pltpu.VMEM((hg, 1, tq), F32), pltpu.VMEM((hg, ACC_ROWS, tq), F32)],
        compiler_params=_cparams(("arbitrary", "arbitrary", "arbitrary")),
        name="mla_flash",
    )(q, k, vt)


def _band_kernel(*refs, nkb, use_sink, tq, nb):
    refs = list(refs)
    sink_ref = refs.pop(0) if use_sink else None
    q_ref = refs.pop(0)
    k_refs = [refs.pop(0) for _ in range(nkb)]
    vt_refs = [refs.pop(0) for _ in range(nkb)]
    bias_ref, o_ref = refs
    kl = nkb * tq
    ones = jnp.ones((BF16_ROWS, kl), BF16)

    def scores(bb):
        k = jnp.concatenate([r[bb] for r in k_refs], axis=0)
        return _dot_nt(k, _split_groups(q_ref, tq, bb)) + bias_ref[0]

    def finish(bb, s):
        vt = jnp.concatenate([r[bb, 0] for r in vt_refs], axis=1)
        out_t = []
        for t in range(N_KV):
            ps, ms = [], []
            for p in range(GROUP):
                h = t * GROUP + p
                sp = s[:, h * tq:(h + 1) * tq]
                m = jnp.max(sp, axis=0, keepdims=True)
                if use_sink:
                    m = jnp.maximum(m, sink_ref[h])
                ps.append(jnp.exp2(sp - m).astype(BF16))
                ms.append(m)
            vg = jnp.concatenate([vt[t * HEAD_DIM:(t + 1) * HEAD_DIM, :], ones], axis=0)
            acc = _dot(vg, jnp.concatenate(ps, axis=1))
            for p in range(GROUP):
                den = acc[HEAD_DIM:HEAD_DIM + 1, p * tq:(p + 1) * tq]
                if use_sink:
                    den = den + jnp.exp2(sink_ref[t * GROUP + p] - ms[p])
                out_t.append(acc[:HEAD_DIM, p * tq:(p + 1) * tq] / den)
        o_ref[bb] = _merge_heads(out_t).astype(BF16)

    s = scores(0)
    for bb in range(nb):
        s_next = scores(bb + 1) if bb + 1 < nb else None
        finish(bb, s)
        s = s_next


def _band_attention(q, k, vt, bias, window, sinks=None):
    B, S, W = q.shape
    tq = BAND_TQ
    nkb = -(-(window - 1) // tq) + 1
    nt = S // tq
    nb = next(n for n in (4, 2, 1) if B % n == 0)
    use_sink = sinks is not None
    vt = vt.reshape(B, nt, LANES, tq)

    def k_spec(back):
        return pl.BlockSpec((nb, tq, LANES), lambda i, b: (b, jnp.maximum(i - back, 0), 0))

    def vt_spec(back):
        return pl.BlockSpec((nb, 1, LANES, tq), lambda i, b: (b, jnp.maximum(i - back, 0), 0, 0))

    in_specs = ([pl.BlockSpec(memory_space=pltpu.SMEM)] if use_sink else []) + (
        [pl.BlockSpec((nb, tq, W), lambda i, b: (b, i, 0))]
        + [k_spec(nkb - 1 - n) for n in range(nkb)] + [vt_spec(nkb - 1 - n) for n in range(nkb)]
        + [pl.BlockSpec((1, nkb * tq, N_HEADS * tq), lambda i, b: (i, 0, 0))])
    args = ([sinks] if use_sink else []) + [q] + [k] * nkb + [vt] * nkb + [bias]
    return pl.pallas_call(
        functools.partial(_band_kernel, nkb=nkb, use_sink=use_sink, tq=tq, nb=nb),
        grid=(nt, B // nb),
        in_specs=in_specs,
        out_specs=pl.BlockSpec((nb, tq, W), lambda i, b: (b, i, 0)),
        out_shape=jax.ShapeDtypeStruct((B, S, W), BF16),
        compiler_params=_cparams(("arbitrary", "arbitrary")),
        name="band_attention_w%d" % window,
    )(*args)


def _mem_kernel(qm_ref, km_ref, vmt_ref, om_ref):
    blocks = []
    for h in range(MEM_HEADS):
        sl = slice(h * MEM_HEAD_DIM, (h + 1) * MEM_HEAD_DIM)
        sm = _dot_nt(km_ref[0, :, sl], qm_ref[0, :, sl])
        pr = jnp.exp2(sm - jnp.max(sm, axis=0, keepdims=True))
        om = _dot(vmt_ref[0, sl, :], pr.astype(BF16)) / jnp.sum(pr, axis=0, keepdims=True)
        blocks.append(om.T)
    om_ref[0] = jnp.concatenate(blocks, axis=1).astype(BF16)


def _mem_attention(qm, km, vmt):
    B, S, _ = qm.shape
    M = km.shape[1]
    tq = min(MEM_TQ, S)
    return pl.pallas_call(
        _mem_kernel,
        grid=(B, S // tq),
        in_specs=[pl.BlockSpec((1, tq, MEM_W), lambda b, i: (b, i, 0)),
                  pl.BlockSpec((1, M, MEM_W), lambda b, i: (b, 0, 0)),
                  pl.BlockSpec((1, MEM_W, M), lambda b, i: (b, 0, 0))],
        out_specs=pl.BlockSpec((1, tq, MEM_W), lambda b, i: (b, i, 0)),
        out_shape=jax.ShapeDtypeStruct((B, S, MEM_W), BF16),
        compiler_params=_cparams(("arbitrary", "arbitrary")),
        name="mem_attention",
    )(qm, km, vmt)


def _gate(o_ref, g_ref):
    return (o_ref[...].astype(F32) * _silu(g_ref[...].astype(F32))).astype(BF16)


def _out_even_kernel(x_ref, o1, g1, o2, g2, o3, g3, w_ref, gpost_ref, y_ref):
    W = o1.shape[1]
    y = (_dot(_gate(o1, g1), w_ref[0:W, :]) + _dot(_gate(o2, g2), w_ref[W:2 * W, :])
         + _dot(_gate(o3, g3), w_ref[2 * W:3 * W, :]))
    y_ref[...] = x_ref[...] + _rms(y, gpost_ref[...])


def _out_even(x2, o1, g1, o2, g2, o3, g3, w, gpost):
    T, D = x2.shape
    tm = PROJ_TM
    row = lambda n: pl.BlockSpec((tm, n), lambda i: (i, 0))
    return pl.pallas_call(
        _out_even_kernel,
        grid=(T // tm,),
        in_specs=[row(D)] + [row(512)] * 6 + [_full(w.shape), _full((1, D))],
        out_specs=row(D),
        out_shape=jax.ShapeDtypeStruct((T, D), F32),
        compiler_params=_cparams(("arbitrary",)),
        name="out_proj_even",
    )(x2, o1, g1, o2, g2, o3, g3, w, gpost)


def _out_odd_kernel(x_ref, oc, os_, ow, gl_ref, gn, om, gm, ex_ref, w_ref, gpost_ref, y_ref):
    W = oc.shape[1]
    sig = jax.nn.sigmoid(gl_ref[...])
    hi = sig.astype(BF16)
    lo = (sig - hi.astype(F32)).astype(BF16)
    mix = jnp.zeros(oc.shape, F32)
    for br, o_ref in enumerate((oc, os_, ow)):
        g = _dot(hi, ex_ref[br]) + _dot(lo, ex_ref[br])
        mix = mix + g * o_ref[...].astype(F32)
    nsa = (mix * _silu(gn[...].astype(F32))).astype(BF16)
    y = _dot(nsa, w_ref[0:W, :]) + _dot(_gate(om, gm), w_ref[W:2 * W, :])
    y_ref[...] = x_ref[...] + _rms(y, gpost_ref[...])


def _out_odd(x2, oc, os_, ow, gl, gn, om, gm, ex, w, gpost):
    T, D = x2.shape
    tm = PROJ_TM
    row = lambda n: pl.BlockSpec((tm, n), lambda i: (i, 0))
    return pl.pallas_call(
        _out_odd_kernel,
        grid=(T // tm,),
        in_specs=[row(D), row(512), row(512), row(512), row(LANES), row(512), row(512), row(512),
                  _full(ex.shape), _full(w.shape), _full((1, D))],
        out_specs=row(D),
        out_shape=jax.ShapeDtypeStruct((T, D), F32),
        compiler_params=_cparams(("arbitrary",)),
        name="out_proj_odd",
    )(x2, oc, os_, ow, gl, gn, om, gm, ex, w, gpost)


O_Q, O_KC, O_VC, O_KS, O_KW, O_GL, O_GN, O_QM, O_GM, O_END = (
    0, 512, 640, 768, 896, 1024, 1152, 1664, 2176, 2688)


def _in_odd_kernel(x_ref, gpre_ref, w_ref, wvs_ref, wvw_ref, q_ref, kc_ref, vc_ref, ks_ref, vs_ref, kw_ref, vw_ref,
                   gl_ref, gn_ref, qm_ref, gm_ref):
    h = _rms(x_ref[...], gpre_ref[...]).astype(BF16)

    def proj(lo, hi):
        return _dot(h, w_ref[:, lo:hi])

    q_ref[...] = (proj(O_Q, O_KC) * (LOG2E * HEAD_DIM ** -0.5)).astype(BF16)
    kc_ref[...] = proj(O_KC, O_VC)
    vc_ref[...] = proj(O_VC, O_KS)
    ks_ref[...] = proj(O_KS, O_KW).astype(BF16)
    vs_ref[0] = _dot_nt(wvs_ref[...], h).astype(BF16)
    kw_ref[...] = proj(O_KW, O_GL).astype(BF16)
    _store_lane_tiles(vw_ref, _dot_nt(wvw_ref[...], h).astype(BF16))
    gl_ref[...] = proj(O_GL, O_GN)
    gn_ref[...] = proj(O_GN, O_QM).astype(BF16)
    qm_ref[...] = (proj(O_QM, O_GM) * (LOG2E * MEM_HEAD_DIM ** -0.5)).astype(BF16)
    gm_ref[...] = proj(O_GM, O_END).astype(BF16)


def _odd_weights(w_in):
    D = w_in.shape[0]
    c = lambda lo, hi: w_in[:, lo:hi]
    w = jnp.concatenate([
        _perm_heads(c(0, 512), 1),
        c(512, 640), c(640, 768), c(768, 896), c(1024, 1152),
        c(1280, 1304), jnp.zeros((D, LANES - 3 * N_HEADS), F32),
        _perm_heads(c(1304, 1816), 1),
        c(1816, 2328), c(2328, 2840)], axis=1).astype(BF16)
    return w, c(896, 1024).T.astype(BF16), c(1152, 1280).T.astype(BF16)


def _in_odd(x2, gpre, w, wvs_t, wvw_t):
    T, D = x2.shape
    tm = PROJ_TM
    row = lambda n: pl.BlockSpec((tm, n), lambda i: (i, 0))
    widths = [512, 128, 128, 128, 128, 128, 128, 128, 512, 512, 512]
    dtypes = [BF16, F32, F32, BF16, BF16, BF16, BF16, F32, BF16, BF16, BF16]
    out_specs = [row(n) for n in widths]
    out_shape = [jax.ShapeDtypeStruct((T, n), dt) for n, dt in zip(widths, dtypes)]
    out_specs[4] = pl.BlockSpec((1, LANES, tm), lambda i: (i, 0, 0))
    out_shape[4] = jax.ShapeDtypeStruct((T // tm, LANES, tm), BF16)
    out_specs[6] = pl.BlockSpec((tm // LANES, LANES, LANES), lambda i: (i, 0, 0))
    out_shape[6] = jax.ShapeDtypeStruct((T // LANES, LANES, LANES), BF16)
    return pl.pallas_call(
        _in_odd_kernel,
        grid=(T // tm,),
        in_specs=[row(D), _full((1, D)), _full(w.shape), _full(wvs_t.shape), _full(wvw_t.shape)],
        out_specs=out_specs,
        out_shape=out_shape,
        compiler_params=_cparams(("arbitrary",)),
        name="in_proj_odd",
    )(x2, gpre, w, wvs_t, wvw_t)


def _compress_kernel(rk_ref, rv_ref, pek_ref, pev_ref, w1k_ref, w1v_ref, w2k_ref, w2vt_ref, kc_ref, vct_ref, *, nc):
    def hidden(r_ref, pe_ref, w1_ref):
        r = r_ref[0]
        za = _dot((r + pe_ref[0:1, :]).astype(BF16), w1_ref[0])
        zb = _dot((r + pe_ref[1:2, :]).astype(BF16), w1_ref[1])
        return _silu(za + pltpu.roll(zb, r.shape[0] - 1, 0)).astype(BF16)

    kc = _dot(hidden(rk_ref, pek_ref, w1k_ref), w2k_ref[...])
    kc_ref[0] = jnp.where(lax.broadcasted_iota(jnp.int32, kc.shape, 0) < nc, kc, 0.0).astype(BF16)
    vct = _dot_nt(w2vt_ref[...], hidden(rv_ref, pev_ref, w1v_ref))
    vct_ref[0] = jnp.where(lax.broadcasted_iota(jnp.int32, vct.shape, 1) < nc, vct, 0.0).astype(BF16)


def _compress_weights(pe, w1, w2):
    eye = jnp.eye(N_KV, dtype=F32)
    d = HEAD_DIM
    w1h = w1.reshape(2, CMP_STRIDE, d, CMP_HIDDEN)
    w1x = jnp.einsum('ztdn,gh->ztgdhn', w1h, eye).reshape(2, CMP_STRIDE * N_KV * d, N_KV * CMP_HIDDEN)
    w2x = jnp.einsum('kn,gh->gkhn', w2, eye).reshape(N_KV * CMP_HIDDEN, N_KV * d)
    pex = jnp.broadcast_to(pe.reshape(2, CMP_STRIDE, 1, d), (2, CMP_STRIDE, N_KV, d)).reshape(2, -1)
    return pex, w1x.astype(BF16), w2x.astype(BF16)


def _compress(rk, rv, pek, pev, w1k, w1v, w2k, w2vt):
    B, n, W = rk.shape
    spec3 = lambda a: pl.BlockSpec(a.shape, lambda b: (0,) * a.ndim)
    rspec = pl.BlockSpec((1, n, W), lambda b: (b, 0, 0))
    return pl.pallas_call(
        functools.partial(_compress_kernel, nc=n - 1),
        grid=(B,),
        in_specs=[rspec, rspec, spec3(pek), spec3(pev), spec3(w1k), spec3(w1v), spec3(w2k), spec3(w2vt)],
        out_specs=[pl.BlockSpec((1, n, LANES), lambda b: (b, 0, 0)), pl.BlockSpec((1, LANES, n), lambda b: (b, 0, 0))],
        out_shape=[jax.ShapeDtypeStruct((B, n, LANES), BF16), jax.ShapeDtypeStruct((B, LANES, n), BF16)],
        compiler_params=_cparams(("arbitrary",)),
        name="compress_blocks",
    )(rk, rv, pek, pev, w1k, w1v, w2k, w2vt)


def _cmp_kernel(q_ref, kc_ref, vct_ref, bias_ref, mt_ref, o_ref, sel_ref, *, tq, ns, top_n):
    i = pl.program_id(0)
    n = kc_ref.shape[1]
    s = _dot_nt(kc_ref[0], _split_groups(q_ref, tq)) + bias_ref[0]
    any_visible = (lax.broadcasted_iota(jnp.int32, (1, tq), 1) + i * tq) >= CMP_LEN - 1

    out_t, psums = [], []
    for t in range(N_KV):
        ps = []
        psum = jnp.zeros((n, tq), F32)
        for p in range(GROUP):
            h = t * GROUP + p
            sp = s[:, h * tq:(h + 1) * tq]
            pr = jnp.exp2(sp - jnp.max(sp, axis=0, keepdims=True))
            den = jnp.sum(pr, axis=0, keepdims=True)
            pr = pr * jnp.where(any_visible, 1.0 / den, 0.0)
            psum = psum + pr
            ps.append(pr.astype(BF16))
        acc = _dot(vct_ref[0, t * HEAD_DIM:(t + 1) * HEAD_DIM, :], jnp.concatenate(ps, axis=1))
        out_t += [acc[:, p * tq:(p + 1) * tq] for p in range(GROUP)]
        psums.append(psum)
    o_ref[0] = _merge_heads(out_t).astype(BF16)

    gq = N_KV * tq
    blk = lax.broadcasted_iota(jnp.int32, (LANES, gq), 0)
    col = lax.broadcasted_iota(jnp.int32, (LANES, gq), 1)
    q_idx = jnp.where(col < tq, col, col - tq) + i * tq
    cur = q_idx // SEL_LEN
    forced = (blk == 0) | (blk == cur) | (blk == cur - 1)
    causal = blk * SEL_LEN <= q_idx
    psum = jnp.concatenate(psums, axis=1)
    hi = psum.astype(BF16)
    lo = (psum - hi.astype(F32)).astype(BF16)
    imp = _dot(mt_ref[...], hi) + _dot(mt_ref[...], lo)
    score = jnp.where(forced, SEL_FORCE, jnp.where(causal, imp, -SEL_FORCE))
    score = jnp.where(blk < ns, score, LOWEST)
    for _ in range(top_n):
        mx = jnp.max(score, axis=0, keepdims=True)
        idx = jnp.min(jnp.where(score == mx, blk, LANES), axis=0, keepdims=True)
        score = jnp.where(blk == idx, LOWEST, score)
    sel = jnp.where(causal & (score == LOWEST) & (blk < ns), 1.0, 0.0)
    for t in range(N_KV):
        sel_ref[0, t] = sel[:, t * tq:(t + 1) * tq].T.astype(BF16)


def _cmp_to_sel_t(n, ns):
    c_start = np.arange(n)[None, :] * CMP_STRIDE
    s_start = np.arange(LANES)[:, None] * SEL_LEN
    overlap = np.maximum(np.minimum(c_start + CMP_LEN, s_start + SEL_LEN) - np.maximum(c_start, s_start), 0)
    m = overlap.astype(np.float32) / CMP_LEN
    m[:, n - 1:] = 0.0
    m[ns:, :] = 0.0
    return jnp.asarray(m, BF16)


def _cmp_select(q, kc, vct, bias, mt):
    B, S, W = q.shape
    n = kc.shape[1]
    tq = bias.shape[2] // N_HEADS
    ns = S // SEL_LEN
    return pl.pallas_call(
        functools.partial(_cmp_kernel, tq=tq, ns=ns, top_n=min(SEL_TOP_N, ns)),
        grid=(S // tq, B),
        in_specs=[pl.BlockSpec((1, tq, W), lambda i, b: (b, i, 0)),
                  pl.BlockSpec((1, n, LANES), lambda i, b: (b, 0, 0)),
                  pl.BlockSpec((1, LANES, n), lambda i, b: (b, 0, 0)),
                  pl.BlockSpec((1, n, N_HEADS * tq), lambda i, b: (i, 0, 0)),
                  pl.BlockSpec(mt.shape, lambda i, b: (0, 0))],
        out_specs=[pl.BlockSpec((1, tq, W), lambda i, b: (b, i, 0)),
                   pl.BlockSpec((1, N_KV, tq, LANES), lambda i, b: (b, 0, i, 0))],
        out_shape=[jax.ShapeDtypeStruct((B, S, W), BF16),
                   jax.ShapeDtypeStruct((B, N_KV, S, LANES), BF16)],
        compiler_params=_cparams(("arbitrary", "arbitrary")),
        name="cmp_select",
    )(q, kc, vct, bias, mt)


def _slc_kernel(qmin_ref, kmax_ref, qmin_sub_ref, kmax_sub_ref, tab_ref, q_ref, sel_ref, pq_ref, pk_ref, k_ref,
                vt_ref, o_ref,
                qs_scr, s_scr, mx_scr, m_scr, acc_scr, *, tq, tk, nvt):
    i = pl.program_id(1)
    m_scr[...] = jnp.full(m_scr.shape, NEG_INF, F32)
    acc_scr[...] = jnp.zeros(acc_scr.shape, F32)
    qs_scr[:, :LANES] = _split_groups(q_ref, tq)
    for t in range(N_KV):
        off = jnp.where(sel_ref[0, t].astype(F32) > 0.5, 0.0, NEG_INF).astype(BF16)
        for p in range(GROUP):
            h = t * GROUP + p
            qs_scr[h * tq:(h + 1) * tq, LANES:] = off
    ones = jnp.ones((BF16_ROWS, tk), BF16)
    gq = GROUP * tq
    last = (i * tq + tq - 1) // tk

    def scores(j, slot):
        start = pl.multiple_of(j * tk, tk)
        blk = lax.broadcasted_iota(jnp.int32, (tk, LANES), 0) // SEL_LEN + j * (tk // SEL_LEN)
        onehot = jnp.where(lax.broadcasted_iota(jnp.int32, (tk, LANES), 1) == blk, 1.0, 0.0).astype(BF16)
        kx = jnp.concatenate([k_ref[0, pl.ds(start, tk), :], onehot], axis=1)
        for t in range(N_KV):
            s = _dot_nt(kx, qs_scr[t * gq:(t + 1) * gq, :])
            s_scr[slot, t] = s
            mx_scr[slot, t] = jnp.max(s, axis=0, keepdims=True)

    def add_bias(j, slot):
        for kb in range(tk // LANES):
            for qb in range(tq // LANES):
                near_sub = (qmin_sub_ref[i * (tq // LANES) + qb] - kmax_sub_ref[j * (tk // LANES) + kb]) < FAR_DIST

                @pl.when(near_sub)
                def _(kb=kb, qb=qb):
                    rows = slice(kb * LANES, (kb + 1) * LANES)
                    dist = pq_ref[0, qb * LANES:(qb + 1) * LANES, :] - pk_ref[j, :, rows]
                    bucket = _t5_bucket(dist.astype(F32).T)
                    for h in range(N_HEADS):
                        t, p = divmod(h, GROUP)
                        cols = slice(p * tq + qb * LANES, p * tq + (qb + 1) * LANES)
                        s_scr[slot, t, rows, cols] = s_scr[slot, t, rows, cols] + _bias_lookup(bucket, tab_ref, h)

    def accumulate(j, slot, near, diag):
        if diag:
            k_idx = lax.broadcasted_iota(jnp.int32, (tk, tq), 0) + j * tk
            q_idx = lax.broadcasted_iota(jnp.int32, (tk, tq), 1) + i * tq
            tok_ok = k_idx <= q_idx
        for t in range(N_KV):
            ps, alphas = [], []
            for p in range(GROUP):
                cols = slice(p * tq, (p + 1) * tq)
                sp = s_scr[slot, t, :, cols]
                if diag:
                    sp = jnp.where(tok_ok, sp, NEG_INF)
                tile_max = jnp.max(sp, axis=0, keepdims=True) if near or diag else mx_scr[slot, t, :, cols]
                m_old = m_scr[t, :, cols]
                m_new = jnp.maximum(m_old, tile_max)
                alphas.append(jnp.exp2(m_old - m_new))
                ps.append(jnp.exp2(sp - m_new).astype(BF16))
                m_scr[t, :, cols] = m_new
            vt = jnp.concatenate([vt_ref[j * nvt + n, t * HEAD_DIM:(t + 1) * HEAD_DIM, :] for n in range(nvt)], axis=1)
            vt = jnp.concatenate([vt, ones], axis=0)
            acc_scr[t] = (jnp.concatenate(alphas, axis=1) * acc_scr[t]
                          + _dot(vt, jnp.concatenate(ps, axis=1)))

    def step(j_next, slot_next, j, slot):
        far = (qmin_ref[i] - kmax_ref[j]) >= FAR_DIST

        @pl.when(far)
        def _():
            scores(j_next, slot_next)
            accumulate(j, slot, False, False)

        @pl.when(jnp.logical_not(far))
        def _():
            add_bias(j, slot)

        @pl.when(jnp.logical_not(far))
        def _():
            scores(j_next, slot_next)
            accumulate(j, slot, True, False)

    def pair(jj, carry):
        step(2 * jj + 1, 1, 2 * jj, 0)
        step(2 * jj + 2, 0, 2 * jj + 1, 1)
        return carry

    scores(0, 0)
    lax.fori_loop(0, last // 2, pair, 0)

    @pl.when(last % 2 == 1)
    def _():
        step(last, 1, last - 1, 0)

    for slot in range(2):
        @pl.when(last % 2 == slot)
        def _(slot=slot):
            add_bias(last, slot)

        @pl.when(last % 2 == slot)
        def _(slot=slot):
            accumulate(last, slot, True, True)

    out_t = []
    for t in range(N_KV):
        for p in range(GROUP):
            cols = slice(p * tq, (p + 1) * tq)
            out_t.append(acc_scr[t, :HEAD_DIM, cols] / acc_scr[t, HEAD_DIM:HEAD_DIM + 1, cols])
    o_ref[0] = _merge_heads(out_t).astype(BF16)


def _slc_attention(q, sel, k, vt, pos, tab_rel):
    B, S, W = q.shape
    tk = min(SLC_TK, S)
    tq = min(SLC_TQ, S)
    nq, nk = S // tq, S // tk
    vtile = vt.shape[-1]
    qmin_sub = pos.reshape(S // LANES, LANES).min(axis=1)
    kmax_sub = pos.reshape(S // LANES, LANES).max(axis=1)
    qmin = qmin_sub.reshape(nq, tq // LANES).min(axis=1)
    kmax = kmax_sub.reshape(nk, tk // LANES).max(axis=1)
    smem = pl.BlockSpec(memory_space=pltpu.SMEM)
    return pl.pallas_call(
        functools.partial(_slc_kernel, tq=tq, tk=tk, nvt=tk // vtile),
        grid=(B, nq),
        in_specs=[smem, smem, smem, smem,
                  pl.BlockSpec(tab_rel.shape, lambda b, i: (0, 0)),
                  pl.BlockSpec((1, tq, W), lambda b, i: (b, i, 0)),
                  pl.BlockSpec((1, N_KV, tq, LANES), lambda b, i: (b, 0, i, 0)),
                  pl.BlockSpec((1, tq, 1), lambda b, i: (i, 0, 0)),
                  pl.BlockSpec((nk, 1, tk), lambda b, i: (0, 0, 0)),
                  pl.BlockSpec((1, S, LANES), lambda b, i: (b, 0, 0)),
                  pl.BlockSpec((S // vtile, LANES, vtile), lambda b, i: (b, 0, 0))],
        out_specs=pl.BlockSpec((1, tq, W), lambda b, i: (b, i, 0)),
        out_shape=jax.ShapeDtypeStruct((B, S, W), BF16),
        scratch_shapes=[pltpu.VMEM((N_HEADS * tq, 2 * LANES), BF16),
                        pltpu.VMEM((2, N_KV, tk, GROUP * tq), F32),
                        pltpu.VMEM((2, N_KV, 1, GROUP * tq), F32),
                        pltpu.VMEM((N_KV, 1, GROUP * tq), F32),
                        pltpu.VMEM((N_KV, ACC_ROWS, GROUP * tq), F32)],
        compiler_params=_cparams(("arbitrary", "arbitrary")),
        name="slc_attention",
    )(qmin, kmax, qmin_sub, kmax_sub, tab_rel, q, sel, pos.reshape(nq, tq, 1), pos.reshape(nk, 1, tk), k, vt)


def _band_positions(pos, tq, pad):
    S = pos.shape[0]
    nq = S // tq
    rows = jnp.pad(pos, (pad, 0)).reshape(nq + pad // tq, tq)
    win = jnp.concatenate([rows[n:n + nq] for n in range(pad // tq + 1)], axis=1)
    return pos.reshape(nq, 1, tq), win.reshape(nq, pad + tq, 1)


def _bias_table(table):
    t = jnp.zeros((N_HEADS, LANES), F32)
    return t.at[:, :NUM_BUCKETS].set(table.astype(F32).T * LOG2E)


def _gate_expand():
    ex = np.zeros((3, LANES, N_HEADS * HEAD_DIM), np.float32)
    for slot, h in enumerate(HEAD_ORDER):
        for br in range(3):
            ex[br, 3 * h + br, slot * HEAD_DIM:(slot + 1) * HEAD_DIM] = 1.0
    return jnp.asarray(ex, BF16)


def kernel(x, mem, positions, rel_bias_table, norm_pre, norm_post, mem_norm, w_mem_kv, w_in_even, mla_q_norm,
           mla_kv_norm, mla_w_uq, mla_w_ukv, swa_sinks, w_out_even, w_in_odd, cmp_pos_k, cmp_pos_v, cmp_w1_k,
           cmp_w2_k, cmp_w1_v, cmp_w2_v, w_out_odd):
    B, S, D = x.shape
    depth = norm_pre.shape[0]
    T = B * S
    pos = positions.astype(jnp.int32)
    n_cmp = S // CMP_STRIDE

    tab = _bias_table(rel_bias_table)
    tab_rel = _bias_table(rel_bias_table - rel_bias_table[NUM_BUCKETS - 1:, :])
    pq, pk_swa = _band_positions(pos, BAND_TQ, BAND_TQ)
    bias_swa = _bias_tiles(tab, pq, pk_swa, SWA_WINDOW)
    _, pk_win = _band_positions(pos, BAND_TQ, (-(-(NSA_WINDOW - 1) // BAND_TQ)) * BAND_TQ)
    bias_win = _bias_tiles(tab, pq, pk_win, NSA_WINDOW)
    pos_cmp = jnp.concatenate([pos[CMP_LEN - 1::CMP_STRIDE], pos[-1:]])
    cmp_tq = min(CMP_TQ, S)
    bias_cmp = _bias_tiles(tab, pos.reshape(S // cmp_tq, 1, cmp_tq), pos_cmp.reshape(1, n_cmp, 1), None)

    ctab, stab = _rope_tables(pos)
    mem_k, mem_vt = _mem_kv(mem, mem_norm, w_mem_kv)
    m_sel_t = _cmp_to_sel_t(n_cmp, S // SEL_LEN)
    gate_ex = _gate_expand()

    x2 = x.reshape(T, D)
    r3 = lambda a: a.reshape(B, S, a.shape[-1])
    for l in range(depth):
        gpre = norm_pre[l].reshape(1, D)
        gpost = norm_post[l].reshape(1, D)
        if l % 2 == 0:
            e = l // 2
            w, wvs_t, wq, wqs, wk, wv = _even_weights(w_in_even[e], mla_w_uq[e], mla_w_ukv[e])
            (qmla, kmla, vmla_t, gmla, qswa, kswa, vswa_t, gswa, qmem, gmem) = _in_even(
                x2, gpre, w, wvs_t, mla_q_norm[e].reshape(1, -1), mla_kv_norm[e].reshape(1, -1),
                wq, wqs, wk, wv, ctab, stab)
            o_mla = _mla_attention(r3(qmla), r3(kmla), vmla_t)
            o_swa = _band_attention(r3(qswa), r3(kswa), vswa_t, bias_swa, SWA_WINDOW, sinks=swa_sinks[e] * LOG2E)
            o_mem = _mem_attention(r3(qmem), mem_k[l], mem_vt[l])
            w_o = w_out_even[e]
            w_o = jnp.concatenate([w_o[:512], _perm_heads(w_o[512:1024], 0), w_o[1024:]], axis=0).astype(BF16)
            x2 = _out_even(x2, o_mla.reshape(T, -1), gmla, o_swa.reshape(T, -1), gswa,
                           o_mem.reshape(T, -1), gmem, w_o, gpost)
        else:
            o = l // 2
            w, wvs_t, wvw_t = _odd_weights(w_in_odd[o])
            (q, kc_in, vc_in, ks, vs_t, kw, vw_t, gl, gn, qmem, gmem) = _in_odd(x2, gpre, w, wvs_t, wvw_t)
            pek, w1k, w2k = _compress_weights(cmp_pos_k[o], cmp_w1_k[o], cmp_w2_k[o])
            pev, w1v, w2v = _compress_weights(cmp_pos_v[o], cmp_w1_v[o], cmp_w2_v[o])
            chunks = lambda a: a.reshape(B, n_cmp, CMP_STRIDE * LANES)
            kc, vc_t = _compress(chunks(kc_in), chunks(vc_in), pek, pev, w1k, w1v, w2k, w2v.T)
            o_cmp, sel = _cmp_select(r3(q), kc, vc_t, bias_cmp, m_sel_t)
            o_slc = _slc_attention(r3(q), sel, r3(ks), vs_t, pos, tab_rel)
            o_win = _band_attention(r3(q), r3(kw), vw_t, bias_win, NSA_WINDOW)
            o_mem = _mem_attention(r3(qmem), mem_k[l], mem_vt[l])
            w_o = w_out_odd[o]
            w_o = jnp.concatenate([_perm_heads(w_o[:512], 0), w_o[512:]], axis=0).astype(BF16)
            x2 = _out_odd(x2, o_cmp.reshape(T, -1), o_slc.reshape(T, -1), o_win.reshape(T, -1), gl, gn,
                          o_mem.reshape(T, -1), gmem, gate_ex, w_o, gpost)
    return x2.reshape(B, S, D)
```

```python
import functools
import math

import numpy as np
import jax
import jax.numpy as jnp
from jax import lax
from jax.experimental import pallas as pl
from jax.experimental.pallas import tpu as pltpu

F32 = jnp.float32
BF16 = jnp.bfloat16

NEG_INF = -1e30
EPS = 1e-6
SEL_FORCE = 1e9
LOWEST = -3.0e38
LOG2E = math.log2(math.e)

NUM_BUCKETS = 32
REL_MAX_DIST = 128

MLA_HEADS = 8
MLA_Q_RANK = 256
MLA_KV_RANK = 128
MLA_NOPE = 64
MLA_ROPE = 32
MLA_V = 64
ROPE_BASE = 10000.0
ROPE_HALF = MLA_ROPE // 2

HEAD_DIM = 64
N_HEADS = 8
N_KV = 2
GROUP = N_HEADS // N_KV
SWA_WINDOW = 128
NSA_WINDOW = 512
CMP_LEN = 32
CMP_STRIDE = 16
CMP_HIDDEN = 256
SEL_LEN = 64
SEL_TOP_N = 16
MEM_HEADS = 4
MEM_HEAD_DIM = 128
MEM_W = MEM_HEADS * MEM_HEAD_DIM

LANES = 128
BF16_ROWS = 16
ACC_ROWS = HEAD_DIM + BF16_ROWS
BAND_TQ = 128
CMP_TQ = 256
MLA_TQ = 512
MLA_TK = 512
MLA_HEADS_PER_STEP = 2
MLA_CHUNK = 512
SLC_TQ = 256
SLC_TK = 512
MEM_TQ = 1024
PROJ_TM = 512
VMEM_LIMIT = 56 * 1024 * 1024


def _t5_far_distance():
    max_exact = NUM_BUCKETS // 2
    d = np.arange(0, 2 * REL_MAX_DIST)
    x = (np.log(np.maximum(d, 1).astype(np.float64) / max_exact)
         / math.log(REL_MAX_DIST / max_exact) * (NUM_BUCKETS - max_exact))
    assert np.abs(x - np.round(x))[max_exact + 1:REL_MAX_DIST].min() > 1e-3
    bucket = np.where(d < max_exact, d, np.minimum(max_exact + np.trunc(x).astype(np.int64), NUM_BUCKETS - 1))
    assert (np.diff(bucket) >= 0).all()
    return int(np.argmax(bucket >= NUM_BUCKETS - 1))


FAR_DIST = _t5_far_distance()

HEAD_ORDER = [h for p in range(GROUP) for h in (p, GROUP + p)]


def _cparams(sem):
    return pltpu.CompilerParams(dimension_semantics=sem, vmem_limit_bytes=VMEM_LIMIT)


def _dot(a, b):
    return jnp.dot(a, b, preferred_element_type=F32)


def _dot_nt(a, b):
    return lax.dot_general(a, b, (((1,), (1,)), ((), ())), preferred_element_type=F32)


def _rms(x, g):
    return x * lax.rsqrt(jnp.mean(x * x, axis=-1, keepdims=True) + EPS) * g


def _silu(x):
    return x * jax.nn.sigmoid(x)


def _t5_bucket(dist):
    max_exact = NUM_BUCKETS // 2
    d = jnp.maximum(dist, 0.0)
    scale = (NUM_BUCKETS - max_exact) / math.log2(REL_MAX_DIST / max_exact)
    large = max_exact + jnp.floor(jnp.log2(jnp.maximum(d, 1.0) * (1.0 / max_exact)) * scale)
    return jnp.where(d < max_exact, d, jnp.minimum(large, NUM_BUCKETS - 1.0)).astype(jnp.int32)


def _bias_lookup(bucket, tab_ref, head):
    rows = jnp.broadcast_to(tab_ref[head:head + 1, :], (bucket.shape[0], LANES))
    return jnp.concatenate(
        [jnp.take_along_axis(rows, bucket[:, c * LANES:(c + 1) * LANES], axis=1)
         for c in range(bucket.shape[1] // LANES)], axis=1)


def _split_groups(q_ref, tq, b=0):
    first = lax.broadcasted_iota(jnp.int32, (tq, LANES), 1) < HEAD_DIM
    parts = []
    for t in range(N_KV):
        for p in range(GROUP):
            qp = q_ref[b, :, p * LANES:(p + 1) * LANES]
            parts.append(jnp.where(first if t == 0 else ~first, qp, jnp.zeros_like(qp)))
    return jnp.concatenate(parts, axis=0)


def _merge_heads(out_t):
    blocks = [jnp.concatenate([out_t[p], out_t[GROUP + p]], axis=0).T for p in range(GROUP)]
    return jnp.concatenate(blocks, axis=1)


def _band_visible(i, kl, tq, window):
    pad = kl - tq
    r = lax.broadcasted_iota(jnp.int32, (kl, tq), 0)
    c = lax.broadcasted_iota(jnp.int32, (kl, tq), 1)
    delta = c + pad - r
    return ((r + (i * tq - pad)) >= 0) & (delta >= 0) & (delta < window)


def _cmp_visible(i, n, tq):
    r = lax.broadcasted_iota(jnp.int32, (n, tq), 0)
    c = lax.broadcasted_iota(jnp.int32, (n, tq), 1)
    return (r * CMP_STRIDE + (CMP_LEN - 1)) <= (c + i * tq)


def _bias_kernel(tab_ref, pq_ref, pk_ref, o_ref, *, tq, window):
    i = pl.program_id(0)
    kl = pk_ref.shape[1]
    bucket = _t5_bucket((pq_ref[0] - pk_ref[0]).astype(F32))
    visible = _cmp_visible(i, kl, tq) if window is None else _band_visible(i, kl, tq, window)
    mask_add = jnp.where(visible, 0.0, NEG_INF)
    for h in range(N_HEADS):
        o_ref[0, :, h * tq:(h + 1) * tq] = _bias_lookup(bucket, tab_ref, h) + mask_add


def _bias_tiles(tab, pos_q, pos_k, window):
    nq, _, tq = pos_q.shape
    kl = pos_k.shape[1]
    per_tile_keys = window is not None
    return pl.pallas_call(
        functools.partial(_bias_kernel, tq=tq, window=window),
        grid=(nq,),
        in_specs=[pl.BlockSpec(tab.shape, lambda i: (0, 0)),
                  pl.BlockSpec((1, 1, tq), lambda i: (i, 0, 0)),
                  pl.BlockSpec((1, kl, 1), (lambda i: (i, 0, 0)) if per_tile_keys else (lambda i: (0, 0, 0)))],
        out_specs=pl.BlockSpec((1, kl, N_HEADS * tq), lambda i: (i, 0, 0)),
        out_shape=jax.ShapeDtypeStruct((nq, kl, N_HEADS * tq), F32),
        compiler_params=_cparams(("arbitrary",)),
        name="bias_tiles",
    )(tab, pos_q, pos_k)


def _memkv_kernel(mem_ref, g_ref, wk_ref, wvt_ref, k_ref, vt_ref):
    mn = _rms(mem_ref[0], g_ref[0]).astype(BF16)
    k_ref[0, 0] = _dot(mn, wk_ref[0]).astype(BF16)
    vt_ref[0, 0] = _dot_nt(wvt_ref[0], mn).astype(BF16)


def _mem_kv(mem, mem_norm, w_mem_kv):
    B, M, D = mem.shape
    L = mem_norm.shape[0]
    wk = w_mem_kv[:, :, :MEM_W].astype(BF16)
    wvt = jnp.swapaxes(w_mem_kv[:, :, MEM_W:], 1, 2).astype(BF16)
    return pl.pallas_call(
        _memkv_kernel,
        grid=(L, B),
        in_specs=[pl.BlockSpec((1, M, D), lambda l, b: (b, 0, 0)),
                  pl.BlockSpec((1, 1, D), lambda l, b: (l, 0, 0)),
                  pl.BlockSpec((1, D, MEM_W), lambda l, b: (l, 0, 0)),
                  pl.BlockSpec((1, MEM_W, D), lambda l, b: (l, 0, 0))],
        out_specs=[pl.BlockSpec((1, 1, M, MEM_W), lambda l, b: (l, b, 0, 0)),
                   pl.BlockSpec((1, 1, MEM_W, M), lambda l, b: (l, b, 0, 0))],
        out_shape=[jax.ShapeDtypeStruct((L, B, M, MEM_W), BF16), jax.ShapeDtypeStruct((L, B, MEM_W, M), BF16)],
        compiler_params=_cparams(("arbitrary", "arbitrary")),
        name="mem_kv",
    )(mem, mem_norm.reshape(L, 1, D), wk, wvt)


E_CQ, E_CKV, E_KR, E_KRS, E_GMLA, E_QSWA, E_KSWA, E_GSWA, E_QMEM, E_GMEM, E_END = (
    0, 256, 384, 512, 640, 1152, 1664, 1792, 2304, 2816, 3328)


def _store_lane_tiles(ref, val):
    for c in range(val.shape[1] // LANES):
        ref[c] = val[:, c * LANES:(c + 1) * LANES]


def _in_even_kernel(x_ref, gpre_ref, w_ref, wvs_ref, qn_ref, kvn_ref, wq_ref, wqs_ref, wk_ref, wv_ref, c_ref, s_ref,
                    qmla_ref, kmla_ref, vmla_ref, gmla_ref, qswa_ref, kswa_ref, vswa_ref, gswa_ref,
                    qmem_ref, gmem_ref):
    h = _rms(x_ref[...], gpre_ref[...]).astype(BF16)

    def proj(lo, hi):
        return _dot(h, w_ref[:, lo:hi])

    cos = c_ref[...]
    sin = s_ref[...]
    cos8 = jnp.concatenate([cos] * MLA_HEADS, axis=1)
    sin8 = jnp.concatenate([sin] * MLA_HEADS, axis=1)

    cq = _rms(proj(E_CQ, E_CKV), qn_ref[...]).astype(BF16)
    q = _dot(cq, wq_ref[...]) * cos8 + _dot(cq, wqs_ref[...]) * sin8
    qmla_ref[...] = (q * (LOG2E * (MLA_NOPE + MLA_ROPE) ** -0.5)).astype(BF16)

    ckv = _rms(proj(E_CKV, E_KR), kvn_ref[...]).astype(BF16)
    kpe = proj(E_KR, E_KRS) * cos + proj(E_KRS, E_GMLA) * sin
    kmla_ref[...] = (_dot(ckv, wk_ref[...]) + jnp.concatenate([kpe] * MLA_HEADS, axis=1)).astype(BF16)
    vmla_ref[0] = _dot_nt(wv_ref[...], ckv).astype(BF16)

    gmla_ref[...] = proj(E_GMLA, E_QSWA).astype(BF16)
    qswa_ref[...] = (proj(E_QSWA, E_KSWA) * (LOG2E * HEAD_DIM ** -0.5)).astype(BF16)
    kswa_ref[...] = proj(E_KSWA, E_GSWA).astype(BF16)
    _store_lane_tiles(vswa_ref, _dot_nt(wvs_ref[...], h).astype(BF16))
    gswa_ref[...] = proj(E_GSWA, E_QMEM).astype(BF16)
    qmem_ref[...] = (proj(E_QMEM, E_GMEM) * (LOG2E * MEM_HEAD_DIM ** -0.5)).astype(BF16)
    gmem_ref[...] = proj(E_GMEM, E_END).astype(BF16)


def _perm_heads(w, axis):
    w = jnp.moveaxis(w, axis, 0)
    rest = w.shape[1:]
    w = w.reshape((N_KV, GROUP, HEAD_DIM) + rest).swapaxes(0, 1).reshape((N_HEADS * HEAD_DIM,) + rest)
    return jnp.moveaxis(w, 0, axis)


def _even_weights(w_in, w_uq, w_ukv):
    D = w_in.shape[0]
    z = lambda n: jnp.zeros((D, n), F32)
    c = lambda lo, hi: w_in[:, lo:hi]
    w = jnp.concatenate([
        c(0, 256), c(256, 384),
        z(MLA_NOPE), c(384, 416), z(32),
        z(MLA_NOPE), c(400, 416), c(384, 400), z(32),
        c(416, 928),
        _perm_heads(c(928, 1440), 1), c(1440, 1568),
        _perm_heads(c(1696, 2208), 1),
        c(2208, 2720), c(2720, 3232)], axis=1).astype(BF16)
    wvs_t = c(1568, 1696).T.astype(BF16)
    hq = MLA_NOPE + MLA_ROPE
    zq = lambda n: jnp.zeros((MLA_Q_RANK, n), F32)
    wq = jnp.concatenate([jnp.concatenate([w_uq[:, h * hq:(h + 1) * hq], zq(32)], axis=1)
                          for h in range(MLA_HEADS)], axis=1).astype(BF16)
    wqs = jnp.concatenate([jnp.concatenate([zq(MLA_NOPE),
                                            w_uq[:, h * hq + MLA_NOPE + ROPE_HALF:(h + 1) * hq],
                                            w_uq[:, h * hq + MLA_NOPE:h * hq + MLA_NOPE + ROPE_HALF],
                                            zq(32)], axis=1)
                           for h in range(MLA_HEADS)], axis=1).astype(BF16)
    hk = MLA_NOPE + MLA_V
    zk = jnp.zeros((MLA_KV_RANK, LANES - MLA_NOPE), F32)
    wk = jnp.concatenate([jnp.concatenate([w_ukv[:, h * hk:h * hk + MLA_NOPE], zk], axis=1)
                          for h in range(MLA_HEADS)], axis=1).astype(BF16)
    wv_t = jnp.concatenate([w_ukv[:, h * hk + MLA_NOPE:(h + 1) * hk] for h in range(MLA_HEADS)],
                           axis=1).T.astype(BF16)
    return w, wvs_t, wq, wqs, wk, wv_t


def _rope_tables(pos):
    inv = ROPE_BASE ** (-jnp.arange(ROPE_HALF, dtype=F32) / ROPE_HALF)
    ang = pos.astype(F32)[:, None] * inv[None, :]
    cos, sin = jnp.cos(ang), jnp.sin(ang)
    S = pos.shape[0]
    ones, z32 = jnp.ones((S, MLA_NOPE), F32), jnp.zeros((S, 32), F32)
    ctab = jnp.concatenate([ones, cos, cos, z32], axis=1)
    stab = jnp.concatenate([jnp.zeros((S, MLA_NOPE), F32), -sin, sin, z32], axis=1)
    return ctab, stab


def _full(shape):
    return pl.BlockSpec(shape, lambda i: (0,) * len(shape))


def _in_even(x2, gpre, w, wvs_t, qn, kvn, wq, wqs, wk, wv, ctab, stab):
    T, D = x2.shape
    S = ctab.shape[0]
    tm = PROJ_TM
    ns = S // tm
    row = lambda n: pl.BlockSpec((tm, n), lambda i: (i, 0))
    widths = [1024, 1024, 512, 512, 512, 128, 128, 512, 512, 512]
    out_specs = [row(n) for n in widths]
    out_shape = [jax.ShapeDtypeStruct((T, n), BF16) for n in widths]
    out_specs[2] = pl.BlockSpec((1, widths[2], tm), lambda i: (i, 0, 0))
    out_shape[2] = jax.ShapeDtypeStruct((T // tm, widths[2], tm), BF16)
    out_specs[6] = pl.BlockSpec((tm // LANES, LANES, LANES), lambda i: (i, 0, 0))
    out_shape[6] = jax.ShapeDtypeStruct((T // LANES, LANES, LANES), BF16)
    return pl.pallas_call(
        _in_even_kernel,
        grid=(T // tm,),
        in_specs=[row(D), _full((1, D)), _full(w.shape), _full(wvs_t.shape), _full((1, MLA_Q_RANK)),
                  _full((1, MLA_KV_RANK)), _full(wq.shape), _full(wqs.shape), _full(wk.shape), _full(wv.shape),
                  pl.BlockSpec((tm, LANES), lambda i: (i % ns, 0)),
                  pl.BlockSpec((tm, LANES), lambda i: (i % ns, 0))],
        out_specs=out_specs,
        out_shape=out_shape,
        compiler_params=_cparams(("arbitrary",)),
        name="in_proj_even",
    )(x2, gpre, w, wvs_t, qn, kvn, wq, wqs, wk, wv, ctab, stab)


def _mla_kernel(q_ref, k_ref, vt_ref, o_ref, s_scr, mx_scr, m_scr, acc_scr, *, tq, tk, nvt, hg):
    i = pl.program_id(2)
    m_scr[...] = jnp.full(m_scr.shape, NEG_INF, F32)
    acc_scr[...] = jnp.zeros(acc_scr.shape, F32)
    ones = jnp.ones((BF16_ROWS, tk), BF16)
    last = (i * tq + tq - 1) // tk

    cq = min(MLA_CHUNK, tq)

    def work(score_tile, acc_tile, diag=False):
        if score_tile is not None:
            js, ss = score_tile
            start = pl.multiple_of(js * tk, tk)
        if acc_tile is not None:
            ja, sa = acc_tile
            if diag:
                key_row = lax.broadcasted_iota(jnp.int32, (tk, cq), 0)
                query_col = lax.broadcasted_iota(jnp.int32, (tk, cq), 1)
                lead = 0 if tq == tk else i * tq - ja * tk
        for t in range(hg):
            if acc_tile is not None:
                vt = jnp.concatenate([vt_ref[ja * nvt + n, t * MLA_V:(t + 1) * MLA_V, :] for n in range(nvt)], axis=1)
                vt = jnp.concatenate([vt, ones], axis=0)
            for c in range(tq // cq):
                cols = slice(c * cq, (c + 1) * cq)
                if score_tile is not None:
                    s = _dot_nt(k_ref[0, pl.ds(start, tk), t * LANES:(t + 1) * LANES],
                                q_ref[0, cols, t * LANES:(t + 1) * LANES])
                    s_scr[ss, t, :, cols] = s
                    mx_scr[ss, t, :, cols] = jnp.max(s, axis=0, keepdims=True)
                if acc_tile is not None:
                    s = s_scr[sa, t, :, cols]
                    if diag:
                        s = jnp.where(key_row <= query_col + (lead + c * cq), s, NEG_INF)
                        tile_max = jnp.max(s, axis=0, keepdims=True)
                    else:
                        tile_max = mx_scr[sa, t, :, cols]
                    m_old = m_scr[t, :, cols]
                    m_new = jnp.maximum(m_old, tile_max)
                    p = jnp.exp2(s - m_new).astype(BF16)
                    m_scr[t, :, cols] = m_new
                    acc_scr[t, :, cols] = jnp.exp2(m_old - m_new) * acc_scr[t, :, cols] + _dot(vt, p)

    def pair(jj, carry):
        work((2 * jj + 1, 1), (2 * jj, 0))
        work((2 * jj + 2, 0), (2 * jj + 1, 1))
        return carry

    work((0, 0), None)
    lax.fori_loop(0, last // 2, pair, 0)

    @pl.when(last % 2 == 0)
    def _():
        work(None, (last, 0), diag=True)

    @pl.when(last % 2 == 1)
    def _():
        work((last, 1), (last - 1, 0))
        work(None, (last, 1), diag=True)

    out_t = jnp.concatenate([acc_scr[t, :MLA_V] / acc_scr[t, MLA_V:MLA_V + 1] for t in range(hg)], axis=0)
    o_ref[0] = out_t.T.astype(BF16)


def _mla_attention(q, k, vt):
    B, S, _ = q.shape
    vtile = vt.shape[-1]
    tq = min(MLA_TQ, S)
    tk = min(MLA_TK, S)
    hg = MLA_HEADS_PER_STEP
    return pl.pallas_call(
        functools.partial(_mla_kernel, tq=tq, tk=tk, nvt=tk // vtile, hg=hg),
        grid=(B, MLA_HEADS // hg, S // tq),
        in_specs=[pl.BlockSpec((1, tq, hg * LANES), lambda b, p, i: (b, i, p)),
                  pl.BlockSpec((1, S, hg * LANES), lambda b, p, i: (b, 0, p)),
                  pl.BlockSpec((S // vtile, hg * MLA_V, vtile), lambda b, p, i: (b, p, 0))],
        out_specs=pl.BlockSpec((1, tq, hg * MLA_V), lambda b, p, i: (b, i, p)),
        out_shape=jax.ShapeDtypeStruct((B, S, MLA_HEADS * MLA_V), BF16),
        scratch_shapes=[pltpu.VMEM((2, hg, tk, tq), F32), pltpu.VMEM((2, hg, 1, tq), F32),
                        pltpu.VMEM((hg, 1, tq), F32), pltpu.VMEM((hg, ACC_ROWS, tq), F32)],
        compiler_params=_cparams(("arbitrary", "arbitrary", "arbitrary")),
        name="mla_flash",
    )(q, k, vt)


def _band_kernel(*refs, nkb, use_sink, tq, nb):
    refs = list(refs)
    sink_ref = refs.pop(0) if use_sink else None
    q_ref = refs.pop(0)
    k_refs = [refs.pop(0) for _ in range(nkb)]
    vt_refs = [refs.pop(0) for _ in range(nkb)]
    bias_ref, o_ref = refs
    kl = nkb * tq
    ones = jnp.ones((BF16_ROWS, kl), BF16)

    def scores(bb):
        k = jnp.concatenate([r[bb] for r in k_refs], axis=0)
        return _dot_nt(k, _split_groups(q_ref, tq, bb)) + bias_ref[0]

    def finish(bb, s):
        vt = jnp.concatenate([r[bb, 0] for r in vt_refs], axis=1)
        out_t = []
        for t in range(N_KV):
            ps, ms = [], []
            for p in range(GROUP):
                h = t * GROUP + p
                sp = s[:, h * tq:(h + 1) * tq]
                m = jnp.max(sp, axis=0, keepdims=True)
                if use_sink:
                    m = jnp.maximum(m, sink_ref[h])
                ps.append(jnp.exp2(sp - m).astype(BF16))
                ms.append(m)
            vg = jnp.concatenate([vt[t * HEAD_DIM:(t + 1) * HEAD_DIM, :], ones], axis=0)
            acc = _dot(vg, jnp.concatenate(ps, axis=1))
            for p in range(GROUP):
                den = acc[HEAD_DIM:HEAD_DIM + 1, p * tq:(p + 1) * tq]
                if use_sink:
                    den = den + jnp.exp2(sink_ref[t * GROUP + p] - ms[p])
                out_t.append(acc[:HEAD_DIM, p * tq:(p + 1) * tq] / den)
        o_ref[bb] = _merge_heads(out_t).astype(BF16)

    s = scores(0)
    for bb in range(nb):
        s_next = scores(bb + 1) if bb + 1 < nb else None
        finish(bb, s)
        s = s_next


def _band_attention(q, k, vt, bias, window, sinks=None):
    B, S, W = q.shape
    tq = BAND_TQ
    nkb = -(-(window - 1) // tq) + 1
    nt = S // tq
    nb = next(n for n in (4, 2, 1) if B % n == 0)
    use_sink = sinks is not None
    vt = vt.reshape(B, nt, LANES, tq)

    def k_spec(back):
        return pl.BlockSpec((nb, tq, LANES), lambda i, b: (b, jnp.maximum(i - back, 0), 0))

    def vt_spec(back):
        return pl.BlockSpec((nb, 1, LANES, tq), lambda i, b: (b, jnp.maximum(i - back, 0), 0, 0))

    in_specs = ([pl.BlockSpec(memory_space=pltpu.SMEM)] if use_sink else []) + (
        [pl.BlockSpec((nb, tq, W), lambda i, b: (b, i, 0))]
        + [k_spec(nkb - 1 - n) for n in range(nkb)] + [vt_spec(nkb - 1 - n) for n in range(nkb)]
        + [pl.BlockSpec((1, nkb * tq, N_HEADS * tq), lambda i, b: (i, 0, 0))])
    args = ([sinks] if use_sink else []) + [q] + [k] * nkb + [vt] * nkb + [bias]
    return pl.pallas_call(
        functools.partial(_band_kernel, nkb=nkb, use_sink=use_sink, tq=tq, nb=nb),
        grid=(nt, B // nb),
        in_specs=in_specs,
        out_specs=pl.BlockSpec((nb, tq, W), lambda i, b: (b, i, 0)),
        out_shape=jax.ShapeDtypeStruct((B, S, W), BF16),
        compiler_params=_cparams(("arbitrary", "arbitrary")),
        name="band_attention_w%d" % window,
    )(*args)


def _mem_kernel(qm_ref, km_ref, vmt_ref, om_ref):
    blocks = []
    for h in range(MEM_HEADS):
        sl = slice(h * MEM_HEAD_DIM, (h + 1) * MEM_HEAD_DIM)
        sm = _dot_nt(km_ref[0, :, sl], qm_ref[0, :, sl])
        pr = jnp.exp2(sm - jnp.max(sm, axis=0, keepdims=True))
        om = _dot(vmt_ref[0, sl, :], pr.astype(BF16)) / jnp.sum(pr, axis=0, keepdims=True)
        blocks.append(om.T)
    om_ref[0] = jnp.concatenate(blocks, axis=1).astype(BF16)


def _mem_attention(qm, km, vmt):
    B, S, _ = qm.shape
    M = km.shape[1]
    tq = min(MEM_TQ, S)
    return pl.pallas_call(
        _mem_kernel,
        grid=(B, S // tq),
        in_specs=[pl.BlockSpec((1, tq, MEM_W), lambda b, i: (b, i, 0)),
                  pl.BlockSpec((1, M, MEM_W), lambda b, i: (b, 0, 0)),
                  pl.BlockSpec((1, MEM_W, M), lambda b, i: (b, 0, 0))],
        out_specs=pl.BlockSpec((1, tq, MEM_W), lambda b, i: (b, i, 0)),
        out_shape=jax.ShapeDtypeStruct((B, S, MEM_W), BF16),
        compiler_params=_cparams(("arbitrary", "arbitrary")),
        name="mem_attention",
    )(qm, km, vmt)


def _gate(o_ref, g_ref):
    return (o_ref[...].astype(F32) * _silu(g_ref[...].astype(F32))).astype(BF16)


def _out_even_kernel(x_ref, o1, g1, o2, g2, o3, g3, w_ref, gpost_ref, y_ref):
    W = o1.shape[1]
    y = (_dot(_gate(o1, g1), w_ref[0:W, :]) + _dot(_gate(o2, g2), w_ref[W:2 * W, :])
         + _dot(_gate(o3, g3), w_ref[2 * W:3 * W, :]))
    y_ref[...] = x_ref[...] + _rms(y, gpost_ref[...])


def _out_even(x2, o1, g1, o2, g2, o3, g3, w, gpost):
    T, D = x2.shape
    tm = PROJ_TM
    row = lambda n: pl.BlockSpec((tm, n), lambda i: (i, 0))
    return pl.pallas_call(
        _out_even_kernel,
        grid=(T // tm,),
        in_specs=[row(D)] + [row(512)] * 6 + [_full(w.shape), _full((1, D))],
        out_specs=row(D),
        out_shape=jax.ShapeDtypeStruct((T, D), F32),
        compiler_params=_cparams(("arbitrary",)),
        name="out_proj_even",
    )(x2, o1, g1, o2, g2, o3, g3, w, gpost)


def _out_odd_kernel(x_ref, oc, os_, ow, gl_ref, gn, om, gm, ex_ref, w_ref, gpost_ref, y_ref):
    W = oc.shape[1]
    sig = jax.nn.sigmoid(gl_ref[...])
    hi = sig.astype(BF16)
    lo = (sig - hi.astype(F32)).astype(BF16)
    mix = jnp.zeros(oc.shape, F32)
    for br, o_ref in enumerate((oc, os_, ow)):
        g = _dot(hi, ex_ref[br]) + _dot(lo, ex_ref[br])
        mix = mix + g * o_ref[...].astype(F32)
    nsa = (mix * _silu(gn[...].astype(F32))).astype(BF16)
    y = _dot(nsa, w_ref[0:W, :]) + _dot(_gate(om, gm), w_ref[W:2 * W, :])
    y_ref[...] = x_ref[...] + _rms(y, gpost_ref[...])


def _out_odd(x2, oc, os_, ow, gl, gn, om, gm, ex, w, gpost):
    T, D = x2.shape
    tm = PROJ_TM
    row = lambda n: pl.BlockSpec((tm, n), lambda i: (i, 0))
    return pl.pallas_call(
        _out_odd_kernel,
        grid=(T // tm,),
        in_specs=[row(D), row(512), row(512), row(512), row(LANES), row(512), row(512), row(512),
                  _full(ex.shape), _full(w.shape), _full((1, D))],
        out_specs=row(D),
        out_shape=jax.ShapeDtypeStruct((T, D), F32),
        compiler_params=_cparams(("arbitrary",)),
        name="out_proj_odd",
    )(x2, oc, os_, ow, gl, gn, om, gm, ex, w, gpost)


O_Q, O_KC, O_VC, O_KS, O_KW, O_GL, O_GN, O_QM, O_GM, O_END = (
    0, 512, 640, 768, 896, 1024, 1152, 1664, 2176, 2688)


def _in_odd_kernel(x_ref, gpre_ref, w_ref, wvs_ref, wvw_ref, q_ref, kc_ref, vc_ref, ks_ref, vs_ref, kw_ref, vw_ref,
                   gl_ref, gn_ref, qm_ref, gm_ref):
    h = _rms(x_ref[...], gpre_ref[...]).astype(BF16)

    def proj(lo, hi):
        return _dot(h, w_ref[:, lo:hi])

    q_ref[...] = (proj(O_Q, O_KC) * (LOG2E * HEAD_DIM ** -0.5)).astype(BF16)
    kc_ref[...] = proj(O_KC, O_VC)
    vc_ref[...] = proj(O_VC, O_KS)
    ks_ref[...] = proj(O_KS, O_KW).astype(BF16)
    vs_ref[0] = _dot_nt(wvs_ref[...], h).astype(BF16)
    kw_ref[...] = proj(O_KW, O_GL).astype(BF16)
    _store_lane_tiles(vw_ref, _dot_nt(wvw_ref[...], h).astype(BF16))
    gl_ref[...] = proj(O_GL, O_GN)
    gn_ref[...] = proj(O_GN, O_QM).astype(BF16)
    qm_ref[...] = (proj(O_QM, O_GM) * (LOG2E * MEM_HEAD_DIM ** -0.5)).astype(BF16)
    gm_ref[...] = proj(O_GM, O_END).astype(BF16)


def _odd_weights(w_in):
    D = w_in.shape[0]
    c = lambda lo, hi: w_in[:, lo:hi]
    w = jnp.concatenate([
        _perm_heads(c(0, 512), 1),
        c(512, 640), c(640, 768), c(768, 896), c(1024, 1152),
        c(1280, 1304), jnp.zeros((D, LANES - 3 * N_HEADS), F32),
        _perm_heads(c(1304, 1816), 1),
        c(1816, 2328), c(2328, 2840)], axis=1).astype(BF16)
    return w, c(896, 1024).T.astype(BF16), c(1152, 1280).T.astype(BF16)


def _in_odd(x2, gpre, w, wvs_t, wvw_t):
    T, D = x2.shape
    tm = PROJ_TM
    row = lambda n: pl.BlockSpec((tm, n), lambda i: (i, 0))
    widths = [512, 128, 128, 128, 128, 128, 128, 128, 512, 512, 512]
    dtypes = [BF16, F32, F32, BF16, BF16, BF16, BF16, F32, BF16, BF16, BF16]
    out_specs = [row(n) for n in widths]
    out_shape = [jax.ShapeDtypeStruct((T, n), dt) for n, dt in zip(widths, dtypes)]
    out_specs[4] = pl.BlockSpec((1, LANES, tm), lambda i: (i, 0, 0))
    out_shape[4] = jax.ShapeDtypeStruct((T // tm, LANES, tm), BF16)
    out_specs[6] = pl.BlockSpec((tm // LANES, LANES, LANES), lambda i: (i, 0, 0))
    out_shape[6] = jax.ShapeDtypeStruct((T // LANES, LANES, LANES), BF16)
    return pl.pallas_call(
        _in_odd_kernel,
        grid=(T // tm,),
        in_specs=[row(D), _full((1, D)), _full(w.shape), _full(wvs_t.shape), _full(wvw_t.shape)],
        out_specs=out_specs,
        out_shape=out_shape,
        compiler_params=_cparams(("arbitrary",)),
        name="in_proj_odd",
    )(x2, gpre, w, wvs_t, wvw_t)


def _compress_kernel(rk_ref, rv_ref, pek_ref, pev_ref, w1k_ref, w1v_ref, w2k_ref, w2vt_ref, kc_ref, vct_ref, *, nc):
    def hidden(r_ref, pe_ref, w1_ref):
        r = r_ref[0]
        za = _dot((r + pe_ref[0:1, :]).astype(BF16), w1_ref[0])
        zb = _dot((r + pe_ref[1:2, :]).astype(BF16), w1_ref[1])
        return _silu(za + pltpu.roll(zb, r.shape[0] - 1, 0)).astype(BF16)

    kc = _dot(hidden(rk_ref, pek_ref, w1k_ref), w2k_ref[...])
    kc_ref[0] = jnp.where(lax.broadcasted_iota(jnp.int32, kc.shape, 0) < nc, kc, 0.0).astype(BF16)
    vct = _dot_nt(w2vt_ref[...], hidden(rv_ref, pev_ref, w1v_ref))
    vct_ref[0] = jnp.where(lax.broadcasted_iota(jnp.int32, vct.shape, 1) < nc, vct, 0.0).astype(BF16)


def _compress_weights(pe, w1, w2):
    eye = jnp.eye(N_KV, dtype=F32)
    d = HEAD_DIM
    w1h = w1.reshape(2, CMP_STRIDE, d, CMP_HIDDEN)
    w1x = jnp.einsum('ztdn,gh->ztgdhn', w1h, eye).reshape(2, CMP_STRIDE * N_KV * d, N_KV * CMP_HIDDEN)
    w2x = jnp.einsum('kn,gh->gkhn', w2, eye).reshape(N_KV * CMP_HIDDEN, N_KV * d)
    pex = jnp.broadcast_to(pe.reshape(2, CMP_STRIDE, 1, d), (2, CMP_STRIDE, N_KV, d)).reshape(2, -1)
    return pex, w1x.astype(BF16), w2x.astype(BF16)


def _compress(rk, rv, pek, pev, w1k, w1v, w2k, w2vt):
    B, n, W = rk.shape
    spec3 = lambda a: pl.BlockSpec(a.shape, lambda b: (0,) * a.ndim)
    rspec = pl.BlockSpec((1, n, W), lambda b: (b, 0, 0))
    return pl.pallas_call(
        functools.partial(_compress_kernel, nc=n - 1),
        grid=(B,),
        in_specs=[rspec, rspec, spec3(pek), spec3(pev), spec3(w1k), spec3(w1v), spec3(w2k), spec3(w2vt)],
        out_specs=[pl.BlockSpec((1, n, LANES), lambda b: (b, 0, 0)), pl.BlockSpec((1, LANES, n), lambda b: (b, 0, 0))],
        out_shape=[jax.ShapeDtypeStruct((B, n, LANES), BF16), jax.ShapeDtypeStruct((B, LANES, n), BF16)],
        compiler_params=_cparams(("arbitrary",)),
        name="compress_blocks",
    )(rk, rv, pek, pev, w1k, w1v, w2k, w2vt)


def _cmp_kernel(q_ref, kc_ref, vct_ref, bias_ref, mt_ref, o_ref, sel_ref, *, tq, ns, top_n):
    i = pl.program_id(0)
    n = kc_ref.shape[1]
    s = _dot_nt(kc_ref[0], _split_groups(q_ref, tq)) + bias_ref[0]
    any_visible = (lax.broadcasted_iota(jnp.int32, (1, tq), 1) + i * tq) >= CMP_LEN - 1

    out_t, psums = [], []
    for t in range(N_KV):
        ps = []
        psum = jnp.zeros((n, tq), F32)
        for p in range(GROUP):
            h = t * GROUP + p
            sp = s[:, h * tq:(h + 1) * tq]
            pr = jnp.exp2(sp - jnp.max(sp, axis=0, keepdims=True))
            den = jnp.sum(pr, axis=0, keepdims=True)
            pr = pr * jnp.where(any_visible, 1.0 / den, 0.0)
            psum = psum + pr
            ps.append(pr.astype(BF16))
        acc = _dot(vct_ref[0, t * HEAD_DIM:(t + 1) * HEAD_DIM, :], jnp.concatenate(ps, axis=1))
        out_t += [acc[:, p * tq:(p + 1) * tq] for p in range(GROUP)]
        psums.append(psum)
    o_ref[0] = _merge_heads(out_t).astype(BF16)

    gq = N_KV * tq
    blk = lax.broadcasted_iota(jnp.int32, (LANES, gq), 0)
    col = lax.broadcasted_iota(jnp.int32, (LANES, gq), 1)
    q_idx = jnp.where(col < tq, col, col - tq) + i * tq
    cur = q_idx // SEL_LEN
    forced = (blk == 0) | (blk == cur) | (blk == cur - 1)
    causal = blk * SEL_LEN <= q_idx
    psum = jnp.concatenate(psums, axis=1)
    hi = psum.astype(BF16)
    lo = (psum - hi.astype(F32)).astype(BF16)
    imp = _dot(mt_ref[...], hi) + _dot(mt_ref[...], lo)
    score = jnp.where(forced, SEL_FORCE, jnp.where(causal, imp, -SEL_FORCE))
    score = jnp.where(blk < ns, score, LOWEST)
    for _ in range(top_n):
        mx = jnp.max(score, axis=0, keepdims=True)
        idx = jnp.min(jnp.where(score == mx, blk, LANES), axis=0, keepdims=True)
        score = jnp.where(blk == idx, LOWEST, score)
    sel = jnp.where(causal & (score == LOWEST) & (blk < ns), 1.0, 0.0)
    for t in range(N_KV):
        sel_ref[0, t] = sel[:, t * tq:(t + 1) * tq].T.astype(BF16)


def _cmp_to_sel_t(n, ns):
    c_start = np.arange(n)[None, :] * CMP_STRIDE
    s_start = np.arange(LANES)[:, None] * SEL_LEN
    overlap = np.maximum(np.minimum(c_start + CMP_LEN, s_start + SEL_LEN) - np.maximum(c_start, s_start), 0)
    m = overlap.astype(np.float32) / CMP_LEN
    m[:, n - 1:] = 0.0
    m[ns:, :] = 0.0
    return jnp.asarray(m, BF16)


def _cmp_select(q, kc, vct, bias, mt):
    B, S, W = q.shape
    n = kc.shape[1]
    tq = bias.shape[2] // N_HEADS
    ns = S // SEL_LEN
    return pl.pallas_call(
        functools.partial(_cmp_kernel, tq=tq, ns=ns, top_n=min(SEL_TOP_N, ns)),
        grid=(S // tq, B),
        in_specs=[pl.BlockSpec((1, tq, W), lambda i, b: (b, i, 0)),
                  pl.BlockSpec((1, n, LANES), lambda i, b: (b, 0, 0)),
                  pl.BlockSpec((1, LANES, n), lambda i, b: (b, 0, 0)),
                  pl.BlockSpec((1, n, N_HEADS * tq), lambda i, b: (i, 0, 0)),
                  pl.BlockSpec(mt.shape, lambda i, b: (0, 0))],
        out_specs=[pl.BlockSpec((1, tq, W), lambda i, b: (b, i, 0)),
                   pl.BlockSpec((1, N_KV, tq, LANES), lambda i, b: (b, 0, i, 0))],
        out_shape=[jax.ShapeDtypeStruct((B, S, W), BF16),
                   jax.ShapeDtypeStruct((B, N_KV, S, LANES), BF16)],
        compiler_params=_cparams(("arbitrary", "arbitrary")),
        name="cmp_select",
    )(q, kc, vct, bias, mt)


def _slc_kernel(qmin_ref, kmax_ref, qmin_sub_ref, kmax_sub_ref, tab_ref, q_ref, sel_ref, pq_ref, pk_ref, k_ref,
                vt_ref, o_ref,
                qs_scr, s_scr, mx_scr, m_scr, acc_scr, *, tq, tk, nvt):
    i = pl.program_id(1)
    m_scr[...] = jnp.full(m_scr.shape, NEG_INF, F32)
    acc_scr[...] = jnp.zeros(acc_scr.shape, F32)
    qs_scr[:, :LANES] = _split_groups(q_ref, tq)
    for t in range(N_KV):
        off = jnp.where(sel_ref[0, t].astype(F32) > 0.5, 0.0, NEG_INF).astype(BF16)
        for p in range(GROUP):
            h = t * GROUP + p
            qs_scr[h * tq:(h + 1) * tq, LANES:] = off
    ones = jnp.ones((BF16_ROWS, tk), BF16)
    gq = GROUP * tq
    last = (i * tq + tq - 1) // tk

    def work(score_tile, acc_tile, near=False, diag=False):
        if score_tile is not None:
            js, ss = score_tile
            start = pl.multiple_of(js * tk, tk)
            blk = lax.broadcasted_iota(jnp.int32, (tk, LANES), 0) // SEL_LEN + js * (tk // SEL_LEN)
            onehot = jnp.where(lax.broadcasted_iota(jnp.int32, (tk, LANES), 1) == blk, 1.0, 0.0).astype(BF16)
            kx = jnp.concatenate([k_ref[0, pl.ds(start, tk), :], onehot], axis=1)
        if acc_tile is not None:
            ja, sa = acc_tile
            if diag:
                tok_ok = (lax.broadcasted_iota(jnp.int32, (tk, tq), 0)
                          - lax.broadcasted_iota(jnp.int32, (tk, tq), 1)) <= i * tq - ja * tk
        for t in range(N_KV):
            if acc_tile is not None:
                vt = jnp.concatenate([vt_ref[ja * nvt + n, t * HEAD_DIM:(t + 1) * HEAD_DIM, :] for n in range(nvt)],
                                     axis=1)
                vt = jnp.concatenate([vt, ones], axis=0)
            for p in range(GROUP):
                h = t * GROUP + p
                cols = slice(p * tq, (p + 1) * tq)
                if score_tile is not None:
                    s = _dot_nt(kx, qs_scr[h * tq:(h + 1) * tq, :])
                    s_scr[ss, t, :, cols] = s
                    mx_scr[ss, t, :, cols] = jnp.max(s, axis=0, keepdims=True)
                if acc_tile is not None:
                    sp = s_scr[sa, t, :, cols]
                    if diag:
                        sp = jnp.where(tok_ok, sp, NEG_INF)
                    tile_max = jnp.max(sp, axis=0, keepdims=True) if near or diag else mx_scr[sa, t, :, cols]
                    m_old = m_scr[t, :, cols]
                    m_new = jnp.maximum(m_old, tile_max)
                    pr = jnp.exp2(sp - m_new).astype(BF16)
                    m_scr[t, :, cols] = m_new
                    acc_scr[t, :, cols] = jnp.exp2(m_old - m_new) * acc_scr[t, :, cols] + _dot(vt, pr)

    def add_bias(j, slot):
        for kb in range(tk // LANES):
            for qb in range(tq // LANES):
                near_sub = (qmin_sub_ref[i * (tq // LANES) + qb] - kmax_sub_ref[j * (tk // LANES) + kb]) < FAR_DIST

                @pl.when(near_sub)
                def _(kb=kb, qb=qb):
                    rows = slice(kb * LANES, (kb + 1) * LANES)
                    dist = pq_ref[0, qb * LANES:(qb + 1) * LANES, :] - pk_ref[j, :, rows]
                    bucket = _t5_bucket(dist.astype(F32).T)
                    for h in range(N_HEADS):
                        t, p = divmod(h, GROUP)
                        cols = slice(p * tq + qb * LANES, p * tq + (qb + 1) * LANES)
                        s_scr[slot, t, rows, cols] = s_scr[slot, t, rows, cols] + _bias_lookup(bucket, tab_ref, h)

    def step(j_next, slot_next, j, slot):
        far = (qmin_ref[i] - kmax_ref[j]) >= FAR_DIST

        @pl.when(far)
        def _():
            work((j_next, slot_next), (j, slot))

        @pl.when(jnp.logical_not(far))
        def _():
            add_bias(j, slot)

        @pl.when(jnp.logical_not(far))
        def _():
            work((j_next, slot_next), (j, slot), near=True)

    def pair(jj, carry):
        step(2 * jj + 1, 1, 2 * jj, 0)
        step(2 * jj + 2, 0, 2 * jj + 1, 1)
        return carry

    work((0, 0), None)
    lax.fori_loop(0, last // 2, pair, 0)

    @pl.when(last % 2 == 1)
    def _():
        step(last, 1, last - 1, 0)

    for slot in range(2):
        @pl.when(last % 2 == slot)
        def _(slot=slot):
            add_bias(last, slot)

        @pl.when(last % 2 == slot)
        def _(slot=slot):
            work(None, (last, slot), near=True, diag=True)

    out_t = []
    for t in range(N_KV):
        for p in range(GROUP):
            cols = slice(p * tq, (p + 1) * tq)
            out_t.append(acc_scr[t, :HEAD_DIM, cols] / acc_scr[t, HEAD_DIM:HEAD_DIM + 1, cols])
    o_ref[0] = _merge_heads(out_t).astype(BF16)


def _slc_attention(q, sel, k, vt, pos, tab_rel):
    B, S, W = q.shape
    tk = min(SLC_TK, S)
    tq = min(SLC_TQ, S)
    nq, nk = S // tq, S // tk
    vtile = vt.shape[-1]
    qmin_sub = pos.reshape(S // LANES, LANES).min(axis=1)
    kmax_sub = pos.reshape(S // LANES, LANES).max(axis=1)
    qmin = qmin_sub.reshape(nq, tq // LANES).min(axis=1)
    kmax = kmax_sub.reshape(nk, tk // LANES).max(axis=1)
    smem = pl.BlockSpec(memory_space=pltpu.SMEM)
    return pl.pallas_call(
        functools.partial(_slc_kernel, tq=tq, tk=tk, nvt=tk // vtile),
        grid=(B, nq),
        in_specs=[smem, smem, smem, smem,
                  pl.BlockSpec(tab_rel.shape, lambda b, i: (0, 0)),
                  pl.BlockSpec((1, tq, W), lambda b, i: (b, i, 0)),
                  pl.BlockSpec((1, N_KV, tq, LANES), lambda b, i: (b, 0, i, 0)),
                  pl.BlockSpec((1, tq, 1), lambda b, i: (i, 0, 0)),
                  pl.BlockSpec((nk, 1, tk), lambda b, i: (0, 0, 0)),
                  pl.BlockSpec((1, S, LANES), lambda b, i: (b, 0, 0)),
                  pl.BlockSpec((S // vtile, LANES, vtile), lambda b, i: (b, 0, 0))],
        out_specs=pl.BlockSpec((1, tq, W), lambda b, i: (b, i, 0)),
        out_shape=jax.ShapeDtypeStruct((B, S, W), BF16),
        scratch_shapes=[pltpu.VMEM((N_HEADS * tq, 2 * LANES), BF16),
                        pltpu.VMEM((2, N_KV, tk, GROUP * tq), F32),
                        pltpu.VMEM((2, N_KV, 1, GROUP * tq), F32),
                        pltpu.VMEM((N_KV, 1, GROUP * tq), F32),
                        pltpu.VMEM((N_KV, ACC_ROWS, GROUP * tq), F32)],
        compiler_params=_cparams(("arbitrary", "arbitrary")),
        name="slc_attention",
    )(qmin, kmax, qmin_sub, kmax_sub, tab_rel, q, sel, pos.reshape(nq, tq, 1), pos.reshape(nk, 1, tk), k, vt)


def _band_positions(pos, tq, pad):
    S = pos.shape[0]
    nq = S // tq
    rows = jnp.pad(pos, (pad, 0)).reshape(nq + pad // tq, tq)
    win = jnp.concatenate([rows[n:n + nq] for n in range(pad // tq + 1)], axis=1)
    return pos.reshape(nq, 1, tq), win.reshape(nq, pad + tq, 1)


def _bias_table(table):
    t = jnp.zeros((N_HEADS, LANES), F32)
    return t.at[:, :NUM_BUCKETS].set(table.astype(F32).T * LOG2E)


def _gate_expand():
    ex = np.zeros((3, LANES, N_HEADS * HEAD_DIM), np.float32)
    for slot, h in enumerate(HEAD_ORDER):
        for br in range(3):
            ex[br, 3 * h + br, slot * HEAD_DIM:(slot + 1) * HEAD_DIM] = 1.0
    return jnp.asarray(ex, BF16)


def kernel(x, mem, positions, rel_bias_table, norm_pre, norm_post, mem_norm, w_mem_kv, w_in_even, mla_q_norm,
           mla_kv_norm, mla_w_uq, mla_w_ukv, swa_sinks, w_out_even, w_in_odd, cmp_pos_k, cmp_pos_v, cmp_w1_k,
           cmp_w2_k, cmp_w1_v, cmp_w2_v, w_out_odd):
    B, S, D = x.shape
    depth = norm_pre.shape[0]
    T = B * S
    pos = positions.astype(jnp.int32)
    n_cmp = S // CMP_STRIDE

    tab = _bias_table(rel_bias_table)
    tab_rel = _bias_table(rel_bias_table - rel_bias_table[NUM_BUCKETS - 1:, :])
    pq, pk_swa = _band_positions(pos, BAND_TQ, BAND_TQ)
    bias_swa = _bias_tiles(tab, pq, pk_swa, SWA_WINDOW)
    _, pk_win = _band_positions(pos, BAND_TQ, (-(-(NSA_WINDOW - 1) // BAND_TQ)) * BAND_TQ)
    bias_win = _bias_tiles(tab, pq, pk_win, NSA_WINDOW)
    pos_cmp = jnp.concatenate([pos[CMP_LEN - 1::CMP_STRIDE], pos[-1:]])
    cmp_tq = min(CMP_TQ, S)
    bias_cmp = _bias_tiles(tab, pos.reshape(S // cmp_tq, 1, cmp_tq), pos_cmp.reshape(1, n_cmp, 1), None)

    ctab, stab = _rope_tables(pos)
    mem_k, mem_vt = _mem_kv(mem, mem_norm, w_mem_kv)
    m_sel_t = _cmp_to_sel_t(n_cmp, S // SEL_LEN)
    gate_ex = _gate_expand()

    x2 = x.reshape(T, D)
    r3 = lambda a: a.reshape(B, S, a.shape[-1])
    for l in range(depth):
        gpre = norm_pre[l].reshape(1, D)
        gpost = norm_post[l].reshape(1, D)
        if l % 2 == 0:
            e = l // 2
            w, wvs_t, wq, wqs, wk, wv = _even_weights(w_in_even[e], mla_w_uq[e], mla_w_ukv[e])
            (qmla, kmla, vmla_t, gmla, qswa, kswa, vswa_t, gswa, qmem, gmem) = _in_even(
                x2, gpre, w, wvs_t, mla_q_norm[e].reshape(1, -1), mla_kv_norm[e].reshape(1, -1),
                wq, wqs, wk, wv, ctab, stab)
            o_mla = _mla_attention(r3(qmla), r3(kmla), vmla_t)
            o_swa = _band_attention(r3(qswa), r3(kswa), vswa_t, bias_swa, SWA_WINDOW, sinks=swa_sinks[e] * LOG2E)
            o_mem = _mem_attention(r3(qmem), mem_k[l], mem_vt[l])
            w_o = w_out_even[e]
            w_o = jnp.concatenate([w_o[:512], _perm_heads(w_o[512:1024], 0), w_o[1024:]], axis=0).astype(BF16)
            x2 = _out_even(x2, o_mla.reshape(T, -1), gmla, o_swa.reshape(T, -1), gswa,
                           o_mem.reshape(T, -1), gmem, w_o, gpost)
        else:
            o = l // 2
            w, wvs_t, wvw_t = _odd_weights(w_in_odd[o])
            (q, kc_in, vc_in, ks, vs_t, kw, vw_t, gl, gn, qmem, gmem) = _in_odd(x2, gpre, w, wvs_t, wvw_t)
            pek, w1k, w2k = _compress_weights(cmp_pos_k[o], cmp_w1_k[o], cmp_w2_k[o])
            pev, w1v, w2v = _compress_weights(cmp_pos_v[o], cmp_w1_v[o], cmp_w2_v[o])
            chunks = lambda a: a.reshape(B, n_cmp, CMP_STRIDE * LANES)
            kc, vc_t = _compress(chunks(kc_in), chunks(vc_in), pek, pev, w1k, w1v, w2k, w2v.T)
            o_cmp, sel = _cmp_select(r3(q), kc, vc_t, bias_cmp, m_sel_t)
            o_slc = _slc_attention(r3(q), sel, r3(ks), vs_t, pos, tab_rel)
            o_win = _band_attention(r3(q), r3(kw), vw_t, bias_win, NSA_WINDOW)
            o_mem = _mem_attention(r3(qmem), mem_k[l], mem_vt[l])
            w_o = w_out_odd[o]
            w_o = jnp.concatenate([_perm_heads(w_o[:512], 0), w_o[512:]], axis=0).astype(BF16)
            x2 = _out_odd(x2, o_cmp.reshape(T, -1), o_slc.reshape(T, -1), o_win.reshape(T, -1), gl, gn,
                          o_mem.reshape(T, -1), gmem, gate_ex, w_o, gpost)
    return x2.reshape(B, S, D)
```

```python
import functools
import math

import numpy as np
import jax
import jax.numpy as jnp
from jax import lax
from jax.experimental import pallas as pl
from jax.experimental.pallas import tpu as pltpu

F32 = jnp.float32
BF16 = jnp.bfloat16

NEG_INF = -1e30
EPS = 1e-6
SEL_FORCE = 1e9
LOWEST = -3.0e38
LOG2E = math.log2(math.e)

NUM_BUCKETS = 32
REL_MAX_DIST = 128

MLA_HEADS = 8
MLA_Q_RANK = 256
MLA_KV_RANK = 128
MLA_NOPE = 64
MLA_ROPE = 32
MLA_V = 64
ROPE_BASE = 10000.0
ROPE_HALF = MLA_ROPE // 2

HEAD_DIM = 64
N_HEADS = 8
N_KV = 2
GROUP = N_HEADS // N_KV
SWA_WINDOW = 128
NSA_WINDOW = 512
CMP_LEN = 32
CMP_STRIDE = 16
CMP_HIDDEN = 256
SEL_LEN = 64
SEL_TOP_N = 16
MEM_HEADS = 4
MEM_HEAD_DIM = 128
MEM_W = MEM_HEADS * MEM_HEAD_DIM

LANES = 128
BF16_ROWS = 16
ACC_ROWS = HEAD_DIM + BF16_ROWS
BAND_TQ = 128
CMP_TQ = 256
MLA_TQ = 512
MLA_TK = 512
MLA_HEADS_PER_STEP = 2
MLA_CHUNK = 512
SLC_TQ = 256
SLC_TK = 512
MEM_TQ = 1024
PROJ_TM = 512
VMEM_LIMIT = 56 * 1024 * 1024


def _t5_far_distance():
    max_exact = NUM_BUCKETS // 2
    d = np.arange(0, 2 * REL_MAX_DIST)
    x = (np.log(np.maximum(d, 1).astype(np.float64) / max_exact)
         / math.log(REL_MAX_DIST / max_exact) * (NUM_BUCKETS - max_exact))
    assert np.abs(x - np.round(x))[max_exact + 1:REL_MAX_DIST].min() > 1e-3
    bucket = np.where(d < max_exact, d, np.minimum(max_exact + np.trunc(x).astype(np.int64), NUM_BUCKETS - 1))
    assert (np.diff(bucket) >= 0).all()
    return int(np.argmax(bucket >= NUM_BUCKETS - 1))


FAR_DIST = _t5_far_distance()

HEAD_ORDER = [h for p in range(GROUP) for h in (p, GROUP + p)]


def _cparams(sem):
    return pltpu.CompilerParams(dimension_semantics=sem, vmem_limit_bytes=VMEM_LIMIT)


def _dot(a, b):
    return jnp.dot(a, b, preferred_element_type=F32)


def _dot_nt(a, b):
    return lax.dot_general(a, b, (((1,), (1,)), ((), ())), preferred_element_type=F32)


def _rms(x, g):
    return x * lax.rsqrt(jnp.mean(x * x, axis=-1, keepdims=True) + EPS) * g


def _silu(x):
    return x * jax.nn.sigmoid(x)


def _t5_bucket(dist):
    max_exact = NUM_BUCKETS // 2
    d = jnp.maximum(dist, 0.0)
    scale = (NUM_BUCKETS - max_exact) / math.log2(REL_MAX_DIST / max_exact)
    large = max_exact + jnp.floor(jnp.log2(jnp.maximum(d, 1.0) * (1.0 / max_exact)) * scale)
    return jnp.where(d < max_exact, d, jnp.minimum(large, NUM_BUCKETS - 1.0)).astype(jnp.int32)


def _bias_lookup(bucket, tab_ref, head):
    rows = jnp.broadcast_to(tab_ref[head:head + 1, :], (bucket.shape[0], LANES))
    return jnp.concatenate(
        [jnp.take_along_axis(rows, bucket[:, c * LANES:(c + 1) * LANES], axis=1)
         for c in range(bucket.shape[1] // LANES)], axis=1)


def _split_groups(q_ref, tq, b=0):
    first = lax.broadcasted_iota(jnp.int32, (tq, LANES), 1) < HEAD_DIM
    parts = []
    for t in range(N_KV):
        for p in range(GROUP):
            qp = q_ref[b, :, p * LANES:(p + 1) * LANES]
            parts.append(jnp.where(first if t == 0 else ~first, qp, jnp.zeros_like(qp)))
    return jnp.concatenate(parts, axis=0)


def _merge_heads(out_t):
    blocks = [jnp.concatenate([out_t[p], out_t[GROUP + p]], axis=0).T for p in range(GROUP)]
    return jnp.concatenate(blocks, axis=1)


def _band_visible(i, kl, tq, window):
    pad = kl - tq
    r = lax.broadcasted_iota(jnp.int32, (kl, tq), 0)
    c = lax.broadcasted_iota(jnp.int32, (kl, tq), 1)
    delta = c + pad - r
    return ((r + (i * tq - pad)) >= 0) & (delta >= 0) & (delta < window)


def _cmp_visible(i, n, tq):
    r = lax.broadcasted_iota(jnp.int32, (n, tq), 0)
    c = lax.broadcasted_iota(jnp.int32, (n, tq), 1)
    return (r * CMP_STRIDE + (CMP_LEN - 1)) <= (c + i * tq)


def _bias_kernel(tab_ref, pq_ref, pk_ref, o_ref, *, tq, window):
    i = pl.program_id(0)
    kl = pk_ref.shape[1]
    bucket = _t5_bucket((pq_ref[0] - pk_ref[0]).astype(F32))
    visible = _cmp_visible(i, kl, tq) if window is None else _band_visible(i, kl, tq, window)
    mask_add = jnp.where(visible, 0.0, NEG_INF)
    for h in range(N_HEADS):
        o_ref[0, :, h * tq:(h + 1) * tq] = _bias_lookup(bucket, tab_ref, h) + mask_add


def _bias_tiles(tab, pos_q, pos_k, window):
    nq, _, tq = pos_q.shape
    kl = pos_k.shape[1]
    per_tile_keys = window is not None
    return pl.pallas_call(
        functools.partial(_bias_kernel, tq=tq, window=window),
        grid=(nq,),
        in_specs=[pl.BlockSpec(tab.shape, lambda i: (0, 0)),
                  pl.BlockSpec((1, 1, tq), lambda i: (i, 0, 0)),
                  pl.BlockSpec((1, kl, 1), (lambda i: (i, 0, 0)) if per_tile_keys else (lambda i: (0, 0, 0)))],
        out_specs=pl.BlockSpec((1, kl, N_HEADS * tq), lambda i: (i, 0, 0)),
        out_shape=jax.ShapeDtypeStruct((nq, kl, N_HEADS * tq), F32),
        compiler_params=_cparams(("arbitrary",)),
        name="bias_tiles",
    )(tab, pos_q, pos_k)


def _memkv_kernel(mem_ref, g_ref, wk_ref, wvt_ref, k_ref, vt_ref):
    mn = _rms(mem_ref[0], g_ref[0]).astype(BF16)
    k_ref[0, 0] = _dot(mn, wk_ref[0]).astype(BF16)
    vt_ref[0, 0] = _dot_nt(wvt_ref[0], mn).astype(BF16)


def _mem_kv(mem, mem_norm, w_mem_kv):
    B, M, D = mem.shape
    L = mem_norm.shape[0]
    wk = w_mem_kv[:, :, :MEM_W].astype(BF16)
    wvt = jnp.swapaxes(w_mem_kv[:, :, MEM_W:], 1, 2).astype(BF16)
    return pl.pallas_call(
        _memkv_kernel,
        grid=(L, B),
        in_specs=[pl.BlockSpec((1, M, D), lambda l, b: (b, 0, 0)),
                  pl.BlockSpec((1, 1, D), lambda l, b: (l, 0, 0)),
                  pl.BlockSpec((1, D, MEM_W), lambda l, b: (l, 0, 0)),
                  pl.BlockSpec((1, MEM_W, D), lambda l, b: (l, 0, 0))],
        out_specs=[pl.BlockSpec((1, 1, M, MEM_W), lambda l, b: (l, b, 0, 0)),
                   pl.BlockSpec((1, 1, MEM_W, M), lambda l, b: (l, b, 0, 0))],
        out_shape=[jax.ShapeDtypeStruct((L, B, M, MEM_W), BF16), jax.ShapeDtypeStruct((L, B, MEM_W, M), BF16)],
        compiler_params=_cparams(("arbitrary", "arbitrary")),
        name="mem_kv",
    )(mem, mem_norm.reshape(L, 1, D), wk, wvt)


E_CQ, E_CKV, E_KR, E_KRS, E_GMLA, E_QSWA, E_KSWA, E_GSWA, E_QMEM, E_GMEM, E_END = (
    0, 256, 384, 512, 640, 1152, 1664, 1792, 2304, 2816, 3328)


def _store_lane_tiles(ref, val):
    for c in range(val.shape[1] // LANES):
        ref[c] = val[:, c * LANES:(c + 1) * LANES]


def _in_even_kernel(x_ref, gpre_ref, w_ref, wvs_ref, qn_ref, kvn_ref, wq_ref, wqs_ref, wk_ref, wv_ref, c_ref, s_ref,
                    qmla_ref, kmla_ref, vmla_ref, gmla_ref, qswa_ref, kswa_ref, vswa_ref, gswa_ref,
                    qmem_ref, gmem_ref):
    h = _rms(x_ref[...], gpre_ref[...]).astype(BF16)

    def proj(lo, hi):
        return _dot(h, w_ref[:, lo:hi])

    cos = c_ref[...]
    sin = s_ref[...]
    cos8 = jnp.concatenate([cos] * MLA_HEADS, axis=1)
    sin8 = jnp.concatenate([sin] * MLA_HEADS, axis=1)

    cq = _rms(proj(E_CQ, E_CKV), qn_ref[...]).astype(BF16)
    q = _dot(cq, wq_ref[...]) * cos8 + _dot(cq, wqs_ref[...]) * sin8
    qmla_ref[...] = (q * (LOG2E * (MLA_NOPE + MLA_ROPE) ** -0.5)).astype(BF16)

    ckv = _rms(proj(E_CKV, E_KR), kvn_ref[...]).astype(BF16)
    kpe = proj(E_KR, E_KRS) * cos + proj(E_KRS, E_GMLA) * sin
    kmla_ref[...] = (_dot(ckv, wk_ref[...]) + jnp.concatenate([kpe] * MLA_HEADS, axis=1)).astype(BF16)
    vmla_ref[0] = _dot_nt(wv_ref[...], ckv).astype(BF16)

    gmla_ref[...] = proj(E_GMLA, E_QSWA).astype(BF16)
    qswa_ref[...] = (proj(E_QSWA, E_KSWA) * (LOG2E * HEAD_DIM ** -0.5)).astype(BF16)
    kswa_ref[...] = proj(E_KSWA, E_GSWA).astype(BF16)
    _store_lane_tiles(vswa_ref, _dot_nt(wvs_ref[...], h).astype(BF16))
    gswa_ref[...] = proj(E_GSWA, E_QMEM).astype(BF16)
    qmem_ref[...] = (proj(E_QMEM, E_GMEM) * (LOG2E * MEM_HEAD_DIM ** -0.5)).astype(BF16)
    gmem_ref[...] = proj(E_GMEM, E_END).astype(BF16)


def _perm_heads(w, axis):
    w = jnp.moveaxis(w, axis, 0)
    rest = w.shape[1:]
    w = w.reshape((N_KV, GROUP, HEAD_DIM) + rest).swapaxes(0, 1).reshape((N_HEADS * HEAD_DIM,) + rest)
    return jnp.moveaxis(w, 0, axis)


def _even_weights(w_in, w_uq, w_ukv):
    D = w_in.shape[0]
    z = lambda n: jnp.zeros((D, n), F32)
    c = lambda lo, hi: w_in[:, lo:hi]
    w = jnp.concatenate([
        c(0, 256), c(256, 384),
        z(MLA_NOPE), c(384, 416), z(32),
        z(MLA_NOPE), c(400, 416), c(384, 400), z(32),
        c(416, 928),
        _perm_heads(c(928, 1440), 1), c(1440, 1568),
        _perm_heads(c(1696, 2208), 1),
        c(2208, 2720), c(2720, 3232)], axis=1).astype(BF16)
    wvs_t = c(1568, 1696).T.astype(BF16)
    hq = MLA_NOPE + MLA_ROPE
    zq = lambda n: jnp.zeros((MLA_Q_RANK, n), F32)
    wq = jnp.concatenate([jnp.concatenate([w_uq[:, h * hq:(h + 1) * hq], zq(32)], axis=1)
                          for h in range(MLA_HEADS)], axis=1).astype(BF16)
    wqs = jnp.concatenate([jnp.concatenate([zq(MLA_NOPE),
                                            w_uq[:, h * hq + MLA_NOPE + ROPE_HALF:(h + 1) * hq],
                                            w_uq[:, h * hq + MLA_NOPE:h * hq + MLA_NOPE + ROPE_HALF],
                                            zq(32)], axis=1)
                           for h in range(MLA_HEADS)], axis=1).astype(BF16)
    hk = MLA_NOPE + MLA_V
    zk = jnp.zeros((MLA_KV_RANK, LANES - MLA_NOPE), F32)
    wk = jnp.concatenate([jnp.concatenate([w_ukv[:, h * hk:h * hk + MLA_NOPE], zk], axis=1)
                          for h in range(MLA_HEADS)], axis=1).astype(BF16)
    wv_t = jnp.concatenate([w_ukv[:, h * hk + MLA_NOPE:(h + 1) * hk] for h in range(MLA_HEADS)],
                           axis=1).T.astype(BF16)
    return w, wvs_t, wq, wqs, wk, wv_t


def _rope_tables(pos):
    inv = ROPE_BASE ** (-jnp.arange(ROPE_HALF, dtype=F32) / ROPE_HALF)
    ang = pos.astype(F32)[:, None] * inv[None, :]
    cos, sin = jnp.cos(ang), jnp.sin(ang)
    S = pos.shape[0]
    ones, z32 = jnp.ones((S, MLA_NOPE), F32), jnp.zeros((S, 32), F32)
    ctab = jnp.concatenate([ones, cos, cos, z32], axis=1)
    stab = jnp.concatenate([jnp.zeros((S, MLA_NOPE), F32), -sin, sin, z32], axis=1)
    return ctab, stab


def _full(shape):
    return pl.BlockSpec(shape, lambda i: (0,) * len(shape))


def _in_even(x2, gpre, w, wvs_t, qn, kvn, wq, wqs, wk, wv, ctab, stab):
    T, D = x2.shape
    S = ctab.shape[0]
    tm = PROJ_TM
    ns = S // tm
    row = lambda n: pl.BlockSpec((tm, n), lambda i: (i, 0))
    widths = [1024, 1024, 512, 512, 512, 128, 128, 512, 512, 512]
    out_specs = [row(n) for n in widths]
    out_shape = [jax.ShapeDtypeStruct((T, n), BF16) for n in widths]
    out_specs[2] = pl.BlockSpec((1, widths[2], tm), lambda i: (i, 0, 0))
    out_shape[2] = jax.ShapeDtypeStruct((T // tm, widths[2], tm), BF16)
    out_specs[6] = pl.BlockSpec((tm // LANES, LANES, LANES), lambda i: (i, 0, 0))
    out_shape[6] = jax.ShapeDtypeStruct((T // LANES, LANES, LANES), BF16)
    return pl.pallas_call(
        _in_even_kernel,
        grid=(T // tm,),
        in_specs=[row(D), _full((1, D)), _full(w.shape), _full(wvs_t.shape), _full((1, MLA_Q_RANK)),
                  _full((1, MLA_KV_RANK)), _full(wq.shape), _full(wqs.shape), _full(wk.shape), _full(wv.shape),
                  pl.BlockSpec((tm, LANES), lambda i: (i % ns, 0)),
                  pl.BlockSpec((tm, LANES), lambda i: (i % ns, 0))],
        out_specs=out_specs,
        out_shape=out_shape,
        compiler_params=_cparams(("arbitrary",)),
        name="in_proj_even",
    )(x2, gpre, w, wvs_t, qn, kvn, wq, wqs, wk, wv, ctab, stab)


def _mla_kernel(q_ref, k_ref, vt_ref, o_ref, s_scr, mx_scr, m_scr, acc_scr, *, tq, tk, nvt, hg):
    i = pl.program_id(2)
    m_scr[...] = jnp.full(m_scr.shape, NEG_INF, F32)
    acc_scr[...] = jnp.zeros(acc_scr.shape, F32)
    ones = jnp.ones((BF16_ROWS, tk), BF16)
    last = (i * tq + tq - 1) // tk

    cq = min(MLA_CHUNK, tq)

    def work(score_tile, acc_tile, diag=False):
        if score_tile is not None:
            js, ss = score_tile
            start = pl.multiple_of(js * tk, tk)
        if acc_tile is not None:
            ja, sa = acc_tile
            if diag:
                key_row = lax.broadcasted_iota(jnp.int32, (tk, cq), 0)
                query_col = lax.broadcasted_iota(jnp.int32, (tk, cq), 1)
                lead = 0 if tq == tk else i * tq - ja * tk
        for t in range(hg):
            if acc_tile is not None:
                vt = jnp.concatenate([vt_ref[ja * nvt + n, t * MLA_V:(t + 1) * MLA_V, :] for n in range(nvt)], axis=1)
                vt = jnp.concatenate([vt, ones], axis=0)
            for c in range(tq // cq):
                cols = slice(c * cq, (c + 1) * cq)
                if score_tile is not None:
                    s = _dot_nt(k_ref[0, pl.ds(start, tk), t * LANES:(t + 1) * LANES],
                                q_ref[0, cols, t * LANES:(t + 1) * LANES])
                    s_scr[ss, t, :, cols] = s
                    mx_scr[ss, t, :, cols] = jnp.max(s, axis=0, keepdims=True)
                if acc_tile is not None:
                    s = s_scr[sa, t, :, cols]
                    if diag:
                        s = jnp.where(key_row <= query_col + (lead + c * cq), s, NEG_INF)
                        tile_max = jnp.max(s, axis=0, keepdims=True)
                    else:
                        tile_max = mx_scr[sa, t, :, cols]
                    m_old = m_scr[t, :, cols]
                    m_new = jnp.maximum(m_old, tile_max)
                    p = jnp.exp2(s - m_new).astype(BF16)
                    m_scr[t, :, cols] = m_new
                    acc_scr[t, :, cols] = jnp.exp2(m_old - m_new) * acc_scr[t, :, cols] + _dot(vt, p)

    def pair(jj, carry):
        work((2 * jj + 1, 1), (2 * jj, 0))
        work((2 * jj + 2, 0), (2 * jj + 1, 1))
        return carry

    work((0, 0), None)
    lax.fori_loop(0, last // 2, pair, 0)

    @pl.when(last % 2 == 0)
    def _():
        work(None, (last, 0), diag=True)

    @pl.when(last % 2 == 1)
    def _():
        work((last, 1), (last - 1, 0))
        work(None, (last, 1), diag=True)

    out_t = jnp.concatenate([acc_scr[t, :MLA_V] / acc_scr[t, MLA_V:MLA_V + 1] for t in range(hg)], axis=0)
    o_ref[0] = out_t.T.astype(BF16)


def _mla_attention(q, k, vt):
    B, S, _ = q.shape
    vtile = vt.shape[-1]
    tq = min(MLA_TQ, S)
    tk = min(MLA_TK, S)
    hg = MLA_HEADS_PER_STEP
    return pl.pallas_call(
        functools.partial(_mla_kernel, tq=tq, tk=tk, nvt=tk // vtile, hg=hg),
        grid=(B, MLA_HEADS // hg, S // tq),
        in_specs=[pl.BlockSpec((1, tq, hg * LANES), lambda b, p, i: (b, i, p)),
                  pl.BlockSpec((1, S, hg * LANES), lambda b, p, i: (b, 0, p)),
                  pl.BlockSpec((S // vtile, hg * MLA_V, vtile), lambda b, p, i: (b, p, 0))],
        out_specs=pl.BlockSpec((1, tq, hg * MLA_V), lambda b, p, i: (b, i, p)),
        out_shape=jax.ShapeDtypeStruct((B, S, MLA_HEADS * MLA_V), BF16),
        scratch_shapes=[pltpu.VMEM((2, hg, tk, tq), F32), pltpu.VMEM((2, hg, 1, tq), F32),
                        pltpu.VMEM((hg, 1, tq), F32), pltpu.VMEM((hg, ACC_ROWS, tq), F32)],
        compiler_params=_cparams(("arbitrary", "arbitrary", "arbitrary")),
        name="mla_flash",
    )(q, k, vt)


def _band_kernel(*refs, nkb, use_sink, tq, nb):
    refs = list(refs)
    sink_ref = refs.pop(0) if use_sink else None
    q_ref = refs.pop(0)
    k_refs = [refs.pop(0) for _ in range(nkb)]
    vt_refs = [refs.pop(0) for _ in range(nkb)]
    bias_ref, o_ref = refs
    kl = nkb * tq
    ones = jnp.ones((BF16_ROWS, kl), BF16)

    def scores(bb):
        k = jnp.concatenate([r[bb] for r in k_refs], axis=0)
        return _dot_nt(k, _split_groups(q_ref, tq, bb)) + bias_ref[0]

    def finish(bb, s):
        vt = jnp.concatenate([r[bb, 0] for r in vt_refs], axis=1)
        out_t = []
        for t in range(N_KV):
            ps, ms = [], []
            for p in range(GROUP):
                h = t * GROUP + p
                sp = s[:, h * tq:(h + 1) * tq]
                m = jnp.max(sp, axis=0, keepdims=True)
                if use_sink:
                    m = jnp.maximum(m, sink_ref[h])
                ps.append(jnp.exp2(sp - m).astype(BF16))
                ms.append(m)
            vg = jnp.concatenate([vt[t * HEAD_DIM:(t + 1) * HEAD_DIM, :], ones], axis=0)
            acc = _dot(vg, jnp.concatenate(ps, axis=1))
            for p in range(GROUP):
                den = acc[HEAD_DIM:HEAD_DIM + 1, p * tq:(p + 1) * tq]
                if use_sink:
                    den = den + jnp.exp2(sink_ref[t * GROUP + p] - ms[p])
                out_t.append(acc[:HEAD_DIM, p * tq:(p + 1) * tq] / den)
        o_ref[bb] = _merge_heads(out_t).astype(BF16)

    s = scores(0)
    for bb in range(nb):
        s_next = scores(bb + 1) if bb + 1 < nb else None
        finish(bb, s)
        s = s_next


def _band_attention(q, k, vt, bias, window, sinks=None):
    B, S, W = q.shape
    tq = BAND_TQ
    nkb = -(-(window - 1) // tq) + 1
    nt = S // tq
    nb = next(n for n in (8, 4, 2, 1) if B % n == 0)
    use_sink = sinks is not None
    vt = vt.reshape(B, nt, LANES, tq)

    def k_spec(back):
        return pl.BlockSpec((nb, tq, LANES), lambda i, b: (b, jnp.maximum(i - back, 0), 0))

    def vt_spec(back):
        return pl.BlockSpec((nb, 1, LANES, tq), lambda i, b: (b, jnp.maximum(i - back, 0), 0, 0))

    in_specs = ([pl.BlockSpec(memory_space=pltpu.SMEM)] if use_sink else []) + (
        [pl.BlockSpec((nb, tq, W), lambda i, b: (b, i, 0))]
        + [k_spec(nkb - 1 - n) for n in range(nkb)] + [vt_spec(nkb - 1 - n) for n in range(nkb)]
        + [pl.BlockSpec((1, nkb * tq, N_HEADS * tq), lambda i, b: (i, 0, 0))])
    args = ([sinks] if use_sink else []) + [q] + [k] * nkb + [vt] * nkb + [bias]
    return pl.pallas_call(
        functools.partial(_band_kernel, nkb=nkb, use_sink=use_sink, tq=tq, nb=nb),
        grid=(nt, B // nb),
        in_specs=in_specs,
        out_specs=pl.BlockSpec((nb, tq, W), lambda i, b: (b, i, 0)),
        out_shape=jax.ShapeDtypeStruct((B, S, W), BF16),
        compiler_params=_cparams(("arbitrary", "arbitrary")),
        name="band_attention_w%d" % window,
    )(*args)


def _mem_kernel(qm_ref, km_ref, vmt_ref, om_ref):
    blocks = []
    for h in range(MEM_HEADS):
        sl = slice(h * MEM_HEAD_DIM, (h + 1) * MEM_HEAD_DIM)
        sm = _dot_nt(km_ref[0, :, sl], qm_ref[0, :, sl])
        pr = jnp.exp2(sm - jnp.max(sm, axis=0, keepdims=True))
        om = _dot(vmt_ref[0, sl, :], pr.astype(BF16)) / jnp.sum(pr, axis=0, keepdims=True)
        blocks.append(om.T)
    om_ref[0] = jnp.concatenate(blocks, axis=1).astype(BF16)


def _mem_attention(qm, km, vmt):
    B, S, _ = qm.shape
    M = km.shape[1]
    tq = min(MEM_TQ, S)
    return pl.pallas_call(
        _mem_kernel,
        grid=(B, S // tq),
        in_specs=[pl.BlockSpec((1, tq, MEM_W), lambda b, i: (b, i, 0)),
                  pl.BlockSpec((1, M, MEM_W), lambda b, i: (b, 0, 0)),
                  pl.BlockSpec((1, MEM_W, M), lambda b, i: (b, 0, 0))],
        out_specs=pl.BlockSpec((1, tq, MEM_W), lambda b, i: (b, i, 0)),
        out_shape=jax.ShapeDtypeStruct((B, S, MEM_W), BF16),
        compiler_params=_cparams(("arbitrary", "arbitrary")),
        name="mem_attention",
    )(qm, km, vmt)


def _gate(o_ref, g_ref):
    return (o_ref[...].astype(F32) * _silu(g_ref[...].astype(F32))).astype(BF16)


def _out_even_kernel(x_ref, o1, g1, o2, g2, o3, g3, w_ref, gpost_ref, y_ref):
    W = o1.shape[1]
    y = (_dot(_gate(o1, g1), w_ref[0:W, :]) + _dot(_gate(o2, g2), w_ref[W:2 * W, :])
         + _dot(_gate(o3, g3), w_ref[2 * W:3 * W, :]))
    y_ref[...] = x_ref[...] + _rms(y, gpost_ref[...])


def _out_even(x2, o1, g1, o2, g2, o3, g3, w, gpost):
    T, D = x2.shape
    tm = PROJ_TM
    row = lambda n: pl.BlockSpec((tm, n), lambda i: (i, 0))
    return pl.pallas_call(
        _out_even_kernel,
        grid=(T // tm,),
        in_specs=[row(D)] + [row(512)] * 6 + [_full(w.shape), _full((1, D))],
        out_specs=row(D),
        out_shape=jax.ShapeDtypeStruct((T, D), F32),
        compiler_params=_cparams(("arbitrary",)),
        name="out_proj_even",
    )(x2, o1, g1, o2, g2, o3, g3, w, gpost)


def _out_odd_kernel(x_ref, oc, os_, ow, gl_ref, gn, om, gm, ex_ref, w_ref, gpost_ref, y_ref):
    W = oc.shape[1]
    sig = jax.nn.sigmoid(gl_ref[...])
    hi = sig.astype(BF16)
    lo = (sig - hi.astype(F32)).astype(BF16)
    mix = jnp.zeros(oc.shape, F32)
    for br, o_ref in enumerate((oc, os_, ow)):
        g = _dot(hi, ex_ref[br]) + _dot(lo, ex_ref[br])
        mix = mix + g * o_ref[...].astype(F32)
    nsa = (mix * _silu(gn[...].astype(F32))).astype(BF16)
    y = _dot(nsa, w_ref[0:W, :]) + _dot(_gate(om, gm), w_ref[W:2 * W, :])
    y_ref[...] = x_ref[...] + _rms(y, gpost_ref[...])


def _out_odd(x2, oc, os_, ow, gl, gn, om, gm, ex, w, gpost):
    T, D = x2.shape
    tm = PROJ_TM
    row = lambda n: pl.BlockSpec((tm, n), lambda i: (i, 0))
    return pl.pallas_call(
        _out_odd_kernel,
        grid=(T // tm,),
        in_specs=[row(D), row(512), row(512), row(512), row(LANES), row(512), row(512), row(512),
                  _full(ex.shape), _full(w.shape), _full((1, D))],
        out_specs=row(D),
        out_shape=jax.ShapeDtypeStruct((T, D), F32),
        compiler_params=_cparams(("arbitrary",)),
        name="out_proj_odd",
    )(x2, oc, os_, ow, gl, gn, om, gm, ex, w, gpost)


O_Q, O_KC, O_VC, O_KS, O_KW, O_GL, O_GN, O_QM, O_GM, O_END = (
    0, 512, 640, 768, 896, 1024, 1152, 1664, 2176, 2688)


def _in_odd_kernel(x_ref, gpre_ref, w_ref, wvs_ref, wvw_ref, q_ref, kc_ref, vc_ref, ks_ref, vs_ref, kw_ref, vw_ref,
                   gl_ref, gn_ref, qm_ref, gm_ref):
    h = _rms(x_ref[...], gpre_ref[...]).astype(BF16)

    def proj(lo, hi):
        return _dot(h, w_ref[:, lo:hi])

    q_ref[...] = (proj(O_Q, O_KC) * (LOG2E * HEAD_DIM ** -0.5)).astype(BF16)
    kc_ref[...] = proj(O_KC, O_VC)
    vc_ref[...] = proj(O_VC, O_KS)
    ks_ref[...] = proj(O_KS, O_KW).astype(BF16)
    vs_ref[0] = _dot_nt(wvs_ref[...], h).astype(BF16)
    kw_ref[...] = proj(O_KW, O_GL).astype(BF16)
    _store_lane_tiles(vw_ref, _dot_nt(wvw_ref[...], h).astype(BF16))
    gl_ref[...] = proj(O_GL, O_GN)
    gn_ref[...] = proj(O_GN, O_QM).astype(BF16)
    qm_ref[...] = (proj(O_QM, O_GM) * (LOG2E * MEM_HEAD_DIM ** -0.5)).astype(BF16)
    gm_ref[...] = proj(O_GM, O_END).astype(BF16)


def _odd_weights(w_in):
    D = w_in.shape[0]
    c = lambda lo, hi: w_in[:, lo:hi]
    w = jnp.concatenate([
        _perm_heads(c(0, 512), 1),
        c(512, 640), c(640, 768), c(768, 896), c(1024, 1152),
        c(1280, 1304), jnp.zeros((D, LANES - 3 * N_HEADS), F32),
        _perm_heads(c(1304, 1816), 1),
        c(1816, 2328), c(2328, 2840)], axis=1).astype(BF16)
    return w, c(896, 1024).T.astype(BF16), c(1152, 1280).T.astype(BF16)


def _in_odd(x2, gpre, w, wvs_t, wvw_t):
    T, D = x2.shape
    tm = PROJ_TM
    row = lambda n: pl.BlockSpec((tm, n), lambda i: (i, 0))
    widths = [512, 128, 128, 128, 128, 128, 128, 128, 512, 512, 512]
    dtypes = [BF16, F32, F32, BF16, BF16, BF16, BF16, F32, BF16, BF16, BF16]
    out_specs = [row(n) for n in widths]
    out_shape = [jax.ShapeDtypeStruct((T, n), dt) for n, dt in zip(widths, dtypes)]
    out_specs[4] = pl.BlockSpec((1, LANES, tm), lambda i: (i, 0, 0))
    out_shape[4] = jax.ShapeDtypeStruct((T // tm, LANES, tm), BF16)
    out_specs[6] = pl.BlockSpec((tm // LANES, LANES, LANES), lambda i: (i, 0, 0))
    out_shape[6] = jax.ShapeDtypeStruct((T // LANES, LANES, LANES), BF16)
    return pl.pallas_call(
        _in_odd_kernel,
        grid=(T // tm,),
        in_specs=[row(D), _full((1, D)), _full(w.shape), _full(wvs_t.shape), _full(wvw_t.shape)],
        out_specs=out_specs,
        out_shape=out_shape,
        compiler_params=_cparams(("arbitrary",)),
        name="in_proj_odd",
    )(x2, gpre, w, wvs_t, wvw_t)


def _compress_kernel(rk_ref, rv_ref, pek_ref, pev_ref, w1k_ref, w1v_ref, w2k_ref, w2vt_ref, kc_ref, vct_ref, *, nc):
    def hidden(r_ref, pe_ref, w1_ref):
        r = r_ref[0]
        za = _dot((r + pe_ref[0:1, :]).astype(BF16), w1_ref[0])
        zb = _dot((r + pe_ref[1:2, :]).astype(BF16), w1_ref[1])
        return _silu(za + pltpu.roll(zb, r.shape[0] - 1, 0)).astype(BF16)

    kc = _dot(hidden(rk_ref, pek_ref, w1k_ref), w2k_ref[...])
    kc_ref[0] = jnp.where(lax.broadcasted_iota(jnp.int32, kc.shape, 0) < nc, kc, 0.0).astype(BF16)
    vct = _dot_nt(w2vt_ref[...], hidden(rv_ref, pev_ref, w1v_ref))
    vct_ref[0] = jnp.where(lax.broadcasted_iota(jnp.int32, vct.shape, 1) < nc, vct, 0.0).astype(BF16)


def _compress_weights(pe, w1, w2):
    eye = jnp.eye(N_KV, dtype=F32)
    d = HEAD_DIM
    w1h = w1.reshape(2, CMP_STRIDE, d, CMP_HIDDEN)
    w1x = jnp.einsum('ztdn,gh->ztgdhn', w1h, eye).reshape(2, CMP_STRIDE * N_KV * d, N_KV * CMP_HIDDEN)
    w2x = jnp.einsum('kn,gh->gkhn', w2, eye).reshape(N_KV * CMP_HIDDEN, N_KV * d)
    pex = jnp.broadcast_to(pe.reshape(2, CMP_STRIDE, 1, d), (2, CMP_STRIDE, N_KV, d)).reshape(2, -1)
    return pex, w1x.astype(BF16), w2x.astype(BF16)


def _compress(rk, rv, pek, pev, w1k, w1v, w2k, w2vt):
    B, n, W = rk.shape
    spec3 = lambda a: pl.BlockSpec(a.shape, lambda b: (0,) * a.ndim)
    rspec = pl.BlockSpec((1, n, W), lambda b: (b, 0, 0))
    return pl.pallas_call(
        functools.partial(_compress_kernel, nc=n - 1),
        grid=(B,),
        in_specs=[rspec, rspec, spec3(pek), spec3(pev), spec3(w1k), spec3(w1v), spec3(w2k), spec3(w2vt)],
        out_specs=[pl.BlockSpec((1, n, LANES), lambda b: (b, 0, 0)), pl.BlockSpec((1, LANES, n), lambda b: (b, 0, 0))],
        out_shape=[jax.ShapeDtypeStruct((B, n, LANES), BF16), jax.ShapeDtypeStruct((B, LANES, n), BF16)],
        compiler_params=_cparams(("arbitrary",)),
        name="compress_blocks",
    )(rk, rv, pek, pev, w1k, w1v, w2k, w2vt)


def _cmp_kernel(q_ref, kc_ref, vct_ref, bias_ref, mt_ref, o_ref, sel_ref, *, tq, ns, top_n):
    i = pl.program_id(0)
    n = kc_ref.shape[1]
    s = _dot_nt(kc_ref[0], _split_groups(q_ref, tq)) + bias_ref[0]
    any_visible = (lax.broadcasted_iota(jnp.int32, (1, tq), 1) + i * tq) >= CMP_LEN - 1

    out_t, psums = [], []
    for t in range(N_KV):
        ps = []
        psum = jnp.zeros((n, tq), F32)
        for p in range(GROUP):
            h = t * GROUP + p
            sp = s[:, h * tq:(h + 1) * tq]
            pr = jnp.exp2(sp - jnp.max(sp, axis=0, keepdims=True))
            den = jnp.sum(pr, axis=0, keepdims=True)
            pr = pr * jnp.where(any_visible, 1.0 / den, 0.0)
            psum = psum + pr
            ps.append(pr.astype(BF16))
        acc = _dot(vct_ref[0, t * HEAD_DIM:(t + 1) * HEAD_DIM, :], jnp.concatenate(ps, axis=1))
        out_t += [acc[:, p * tq:(p + 1) * tq] for p in range(GROUP)]
        psums.append(psum)
    o_ref[0] = _merge_heads(out_t).astype(BF16)

    gq = N_KV * tq
    blk = lax.broadcasted_iota(jnp.int32, (LANES, gq), 0)
    col = lax.broadcasted_iota(jnp.int32, (LANES, gq), 1)
    q_idx = jnp.where(col < tq, col, col - tq) + i * tq
    cur = q_idx // SEL_LEN
    forced = (blk == 0) | (blk == cur) | (blk == cur - 1)
    causal = blk * SEL_LEN <= q_idx
    psum = jnp.concatenate(psums, axis=1)
    hi = psum.astype(BF16)
    lo = (psum - hi.astype(F32)).astype(BF16)
    imp = _dot(mt_ref[...], hi) + _dot(mt_ref[...], lo)
    score = jnp.where(forced, SEL_FORCE, jnp.where(causal, imp, -SEL_FORCE))
    score = jnp.where(blk < ns, score, LOWEST)
    for _ in range(top_n):
        mx = jnp.max(score, axis=0, keepdims=True)
        idx = jnp.min(jnp.where(score == mx, blk, LANES), axis=0, keepdims=True)
        score = jnp.where(blk == idx, LOWEST, score)
    sel = jnp.where(causal & (score == LOWEST) & (blk < ns), 1.0, 0.0)
    for t in range(N_KV):
        sel_ref[0, t] = sel[:, t * tq:(t + 1) * tq].T.astype(BF16)


def _cmp_to_sel_t(n, ns):
    c_start = np.arange(n)[None, :] * CMP_STRIDE
    s_start = np.arange(LANES)[:, None] * SEL_LEN
    overlap = np.maximum(np.minimum(c_start + CMP_LEN, s_start + SEL_LEN) - np.maximum(c_start, s_start), 0)
    m = overlap.astype(np.float32) / CMP_LEN
    m[:, n - 1:] = 0.0
    m[ns:, :] = 0.0
    return jnp.asarray(m, BF16)


def _cmp_select(q, kc, vct, bias, mt):
    B, S, W = q.shape
    n = kc.shape[1]
    tq = bias.shape[2] // N_HEADS
    ns = S // SEL_LEN
    return pl.pallas_call(
        functools.partial(_cmp_kernel, tq=tq, ns=ns, top_n=min(SEL_TOP_N, ns)),
        grid=(S // tq, B),
        in_specs=[pl.BlockSpec((1, tq, W), lambda i, b: (b, i, 0)),
                  pl.BlockSpec((1, n, LANES), lambda i, b: (b, 0, 0)),
                  pl.BlockSpec((1, LANES, n), lambda i, b: (b, 0, 0)),
                  pl.BlockSpec((1, n, N_HEADS * tq), lambda i, b: (i, 0, 0)),
                  pl.BlockSpec(mt.shape, lambda i, b: (0, 0))],
        out_specs=[pl.BlockSpec((1, tq, W), lambda i, b: (b, i, 0)),
                   pl.BlockSpec((1, N_KV, tq, LANES), lambda i, b: (b, 0, i, 0))],
        out_shape=[jax.ShapeDtypeStruct((B, S, W), BF16),
                   jax.ShapeDtypeStruct((B, N_KV, S, LANES), BF16)],
        compiler_params=_cparams(("arbitrary", "arbitrary")),
        name="cmp_select",
    )(q, kc, vct, bias, mt)


def _slc_kernel(qmin_ref, kmax_ref, qmin_sub_ref, kmax_sub_ref, tab_ref, q_ref, sel_ref, pq_ref, pk_ref, k_ref,
                vt_ref, o_ref,
                qs_scr, s_scr, mx_scr, m_scr, acc_scr, *, tq, tk, nvt):
    i = pl.program_id(1)
    m_scr[...] = jnp.full(m_scr.shape, NEG_INF, F32)
    acc_scr[...] = jnp.zeros(acc_scr.shape, F32)
    qs_scr[:, :LANES] = _split_groups(q_ref, tq)
    for t in range(N_KV):
        off = jnp.where(sel_ref[0, t].astype(F32) > 0.5, 0.0, NEG_INF).astype(BF16)
        for p in range(GROUP):
            h = t * GROUP + p
            qs_scr[h * tq:(h + 1) * tq, LANES:] = off
    ones = jnp.ones((BF16_ROWS, tk), BF16)
    gq = GROUP * tq
    last = (i * tq + tq - 1) // tk

    def work(score_tile, acc_tile, near=False, diag=False):
        if score_tile is not None:
            js, ss = score_tile
            start = pl.multiple_of(js * tk, tk)
            blk = lax.broadcasted_iota(jnp.int32, (tk, LANES), 0) // SEL_LEN + js * (tk // SEL_LEN)
            onehot = jnp.where(lax.broadcasted_iota(jnp.int32, (tk, LANES), 1) == blk, 1.0, 0.0).astype(BF16)
            kx = jnp.concatenate([k_ref[0, pl.ds(start, tk), :], onehot], axis=1)
        if acc_tile is not None:
            ja, sa = acc_tile
            if diag:
                tok_ok = (lax.broadcasted_iota(jnp.int32, (tk, tq), 0)
                          - lax.broadcasted_iota(jnp.int32, (tk, tq), 1)) <= i * tq - ja * tk
        for t in range(N_KV):
            if acc_tile is not None:
                vt = jnp.concatenate([vt_ref[ja * nvt + n, t * HEAD_DIM:(t + 1) * HEAD_DIM, :] for n in range(nvt)],
                                     axis=1)
                vt = jnp.concatenate([vt, ones], axis=0)
            for p in range(GROUP):
                h = t * GROUP + p
                cols = slice(p * tq, (p + 1) * tq)
                if score_tile is not None:
                    s = _dot_nt(kx, qs_scr[h * tq:(h + 1) * tq, :])
                    s_scr[ss, t, :, cols] = s
                    mx_scr[ss, t, :, cols] = jnp.max(s, axis=0, keepdims=True)
                if acc_tile is not None:
                    sp = s_scr[sa, t, :, cols]
                    if diag:
                        sp = jnp.where(tok_ok, sp, NEG_INF)
                    tile_max = jnp.max(sp, axis=0, keepdims=True) if near or diag else mx_scr[sa, t, :, cols]
                    m_old = m_scr[t, :, cols]
                    m_new = jnp.maximum(m_old, tile_max)
                    pr = jnp.exp2(sp - m_new).astype(BF16)
                    m_scr[t, :, cols] = m_new
                    acc_scr[t, :, cols] = jnp.exp2(m_old - m_new) * acc_scr[t, :, cols] + _dot(vt, pr)

    def add_bias(j, slot):
        for kb in range(tk // LANES):
            for qb in range(tq // LANES):
                near_sub = (qmin_sub_ref[i * (tq // LANES) + qb] - kmax_sub_ref[j * (tk // LANES) + kb]) < FAR_DIST

                @pl.when(near_sub)
                def _(kb=kb, qb=qb):
                    rows = slice(kb * LANES, (kb + 1) * LANES)
                    dist = pq_ref[0, qb * LANES:(qb + 1) * LANES, :] - pk_ref[j, :, rows]
                    bucket = _t5_bucket(dist.astype(F32).T)
                    for h in range(N_HEADS):
                        t, p = divmod(h, GROUP)
                        cols = slice(p * tq + qb * LANES, p * tq + (qb + 1) * LANES)
                        s_scr[slot, t, rows, cols] = s_scr[slot, t, rows, cols] + _bias_lookup(bucket, tab_ref, h)

    def step(j_next, slot_next, j, slot):
        far = (qmin_ref[i] - kmax_ref[j]) >= FAR_DIST

        @pl.when(far)
        def _():
            work((j_next, slot_next), (j, slot))

        @pl.when(jnp.logical_not(far))
        def _():
            add_bias(j, slot)

        @pl.when(jnp.logical_not(far))
        def _():
            work((j_next, slot_next), (j, slot), near=True)

    def pair(jj, carry):
        both_far = (qmin_ref[i] - jnp.maximum(kmax_ref[2 * jj], kmax_ref[2 * jj + 1])) >= FAR_DIST

        @pl.when(both_far)
        def _():
            work((2 * jj + 1, 1), (2 * jj, 0))
            work((2 * jj + 2, 0), (2 * jj + 1, 1))

        @pl.when(jnp.logical_not(both_far))
        def _():
            step(2 * jj + 1, 1, 2 * jj, 0)
            step(2 * jj + 2, 0, 2 * jj + 1, 1)

        return carry

    work((0, 0), None)
    lax.fori_loop(0, last // 2, pair, 0)

    @pl.when(last % 2 == 1)
    def _():
        step(last, 1, last - 1, 0)

    for slot in range(2):
        @pl.when(last % 2 == slot)
        def _(slot=slot):
            add_bias(last, slot)

        @pl.when(last % 2 == slot)
        def _(slot=slot):
            work(None, (last, slot), near=True, diag=True)

    out_t = []
    for t in range(N_KV):
        for p in range(GROUP):
            cols = slice(p * tq, (p + 1) * tq)
            out_t.append(acc_scr[t, :HEAD_DIM, cols] / acc_scr[t, HEAD_DIM:HEAD_DIM + 1, cols])
    o_ref[0] = _merge_heads(out_t).astype(BF16)


def _slc_attention(q, sel, k, vt, pos, tab_rel):
    B, S, W = q.shape
    tk = min(SLC_TK, S)
    tq = min(SLC_TQ, S)
    nq, nk = S // tq, S // tk
    vtile = vt.shape[-1]
    qmin_sub = pos.reshape(S // LANES, LANES).min(axis=1)
    kmax_sub = pos.reshape(S // LANES, LANES).max(axis=1)
    qmin = qmin_sub.reshape(nq, tq // LANES).min(axis=1)
    kmax = kmax_sub.reshape(nk, tk // LANES).max(axis=1)
    smem = pl.BlockSpec(memory_space=pltpu.SMEM)
    return pl.pallas_call(
        functools.partial(_slc_kernel, tq=tq, tk=tk, nvt=tk // vtile),
        grid=(B, nq),
        in_specs=[smem, smem, smem, smem,
                  pl.BlockSpec(tab_rel.shape, lambda b, i: (0, 0)),
                  pl.BlockSpec((1, tq, W), lambda b, i: (b, i, 0)),
                  pl.BlockSpec((1, N_KV, tq, LANES), lambda b, i: (b, 0, i, 0)),
                  pl.BlockSpec((1, tq, 1), lambda b, i: (i, 0, 0)),
                  pl.BlockSpec((nk, 1, tk), lambda b, i: (0, 0, 0)),
                  pl.BlockSpec((1, S, LANES), lambda b, i: (b, 0, 0)),
                  pl.BlockSpec((S // vtile, LANES, vtile), lambda b, i: (b, 0, 0))],
        out_specs=pl.BlockSpec((1, tq, W), lambda b, i: (b, i, 0)),
        out_shape=jax.ShapeDtypeStruct((B, S, W), BF16),
        scratch_shapes=[pltpu.VMEM((N_HEADS * tq, 2 * LANES), BF16),
                        pltpu.VMEM((2, N_KV, tk, GROUP * tq), F32),
                        pltpu.VMEM((2, N_KV, 1, GROUP * tq), F32),
                        pltpu.VMEM((N_KV, 1, GROUP * tq), F32),
                        pltpu.VMEM((N_KV, ACC_ROWS, GROUP * tq), F32)],
        compiler_params=_cparams(("arbitrary", "arbitrary")),
        name="slc_attention",
    )(qmin, kmax, qmin_sub, kmax_sub, tab_rel, q, sel, pos.reshape(nq, tq, 1), pos.reshape(nk, 1, tk), k, vt)


def _band_positions(pos, tq, pad):
    S = pos.shape[0]
    nq = S // tq
    rows = jnp.pad(pos, (pad, 0)).reshape(nq + pad // tq, tq)
    win = jnp.concatenate([rows[n:n + nq] for n in range(pad // tq + 1)], axis=1)
    return pos.reshape(nq, 1, tq), win.reshape(nq, pad + tq, 1)


def _bias_table(table):
    t = jnp.zeros((N_HEADS, LANES), F32)
    return t.at[:, :NUM_BUCKETS].set(table.astype(F32).T * LOG2E)


def _gate_expand():
    ex = np.zeros((3, LANES, N_HEADS * HEAD_DIM), np.float32)
    for slot, h in enumerate(HEAD_ORDER):
        for br in range(3):
            ex[br, 3 * h + br, slot * HEAD_DIM:(slot + 1) * HEAD_DIM] = 1.0
    return jnp.asarray(ex, BF16)


def kernel(x, mem, positions, rel_bias_table, norm_pre, norm_post, mem_norm, w_mem_kv, w_in_even, mla_q_norm,
           mla_kv_norm, mla_w_uq, mla_w_ukv, swa_sinks, w_out_even, w_in_odd, cmp_pos_k, cmp_pos_v, cmp_w1_k,
           cmp_w2_k, cmp_w1_v, cmp_w2_v, w_out_odd):
    B, S, D = x.shape
    depth = norm_pre.shape[0]
    T = B * S
    pos = positions.astype(jnp.int32)
    n_cmp = S // CMP_STRIDE

    tab = _bias_table(rel_bias_table)
    tab_rel = _bias_table(rel_bias_table - rel_bias_table[NUM_BUCKETS - 1:, :])
    pq, pk_swa = _band_positions(pos, BAND_TQ, BAND_TQ)
    bias_swa = _bias_tiles(tab, pq, pk_swa, SWA_WINDOW)
    _, pk_win = _band_positions(pos, BAND_TQ, (-(-(NSA_WINDOW - 1) // BAND_TQ)) * BAND_TQ)
    bias_win = _bias_tiles(tab, pq, pk_win, NSA_WINDOW)
    pos_cmp = jnp.concatenate([pos[CMP_LEN - 1::CMP_STRIDE], pos[-1:]])
    cmp_tq = min(CMP_TQ, S)
    bias_cmp = _bias_tiles(tab, pos.reshape(S // cmp_tq, 1, cmp_tq), pos_cmp.reshape(1, n_cmp, 1), None)

    ctab, stab = _rope_tables(pos)
    mem_k, mem_vt = _mem_kv(mem, mem_norm, w_mem_kv)
    m_sel_t = _cmp_to_sel_t(n_cmp, S // SEL_LEN)
    gate_ex = _gate_expand()

    x2 = x.reshape(T, D)
    r3 = lambda a: a.reshape(B, S, a.shape[-1])
    for l in range(depth):
        gpre = norm_pre[l].reshape(1, D)
        gpost = norm_post[l].reshape(1, D)
        if l % 2 == 0:
            e = l // 2
            w, wvs_t, wq, wqs, wk, wv = _even_weights(w_in_even[e], mla_w_uq[e], mla_w_ukv[e])
            (qmla, kmla, vmla_t, gmla, qswa, kswa, vswa_t, gswa, qmem, gmem) = _in_even(
                x2, gpre, w, wvs_t, mla_q_norm[e].reshape(1, -1), mla_kv_norm[e].reshape(1, -1),
                wq, wqs, wk, wv, ctab, stab)
            o_mla = _mla_attention(r3(qmla), r3(kmla), vmla_t)
            o_swa = _band_attention(r3(qswa), r3(kswa), vswa_t, bias_swa, SWA_WINDOW, sinks=swa_sinks[e] * LOG2E)
            o_mem = _mem_attention(r3(qmem), mem_k[l], mem_vt[l])
            w_o = w_out_even[e]
            w_o = jnp.concatenate([w_o[:512], _perm_heads(w_o[512:1024], 0), w_o[1024:]], axis=0).astype(BF16)
            x2 = _out_even(x2, o_mla.reshape(T, -1), gmla, o_swa.reshape(T, -1), gswa,
                           o_mem.reshape(T, -1), gmem, w_o, gpost)
        else:
            o = l // 2
            w, wvs_t, wvw_t = _odd_weights(w_in_odd[o])
            (q, kc_in, vc_in, ks, vs_t, kw, vw_t, gl, gn, qmem, gmem) = _in_odd(x2, gpre, w, wvs_t, wvw_t)
            pek, w1k, w2k = _compress_weights(cmp_pos_k[o], cmp_w1_k[o], cmp_w2_k[o])
            pev, w1v, w2v = _compress_weights(cmp_pos_v[o], cmp_w1_v[o], cmp_w2_v[o])
            chunks = lambda a: a.reshape(B, n_cmp, CMP_STRIDE * LANES)
            kc, vc_t = _compress(chunks(kc_in), chunks(vc_in), pek, pev, w1k, w1v, w2k, w2v.T)
            o_cmp, sel = _cmp_select(r3(q), kc, vc_t, bias_cmp, m_sel_t)
            o_slc = _slc_attention(r3(q), sel, r3(ks), vs_t, pos, tab_rel)
            o_win = _band_attention(r3(q), r3(kw), vw_t, bias_win, NSA_WINDOW)
            o_mem = _mem_attention(r3(qmem), mem_k[l], mem_vt[l])
            w_o = w_out_odd[o]
            w_o = jnp.concatenate([_perm_heads(w_o[:512], 0), w_o[512:]], axis=0).astype(BF16)
            x2 = _out_odd(x2, o_cmp.reshape(T, -1), o_slc.reshape(T, -1), o_win.reshape(T, -1), gl, gn,
                          o_mem.reshape(T, -1), gmem, gate_ex, w_o, gpost)
    return x2.reshape(B, S, D)
```

```python
import functools
import math

import numpy as np
import jax
import jax.numpy as jnp
from jax import lax
from jax.experimental import pallas as pl
from jax.experimental.pallas import tpu as pltpu

F32 = jnp.float32
BF16 = jnp.bfloat16

NEG_INF = -1e30
EPS = 1e-6
SEL_FORCE = 1e9
LOWEST = -3.0e38
LOG2E = math.log2(math.e)

NUM_BUCKETS = 32
REL_MAX_DIST = 128

MLA_HEADS = 8
MLA_Q_RANK = 256
MLA_KV_RANK = 128
MLA_NOPE = 64
MLA_ROPE = 32
MLA_V = 64
ROPE_BASE = 10000.0
ROPE_HALF = MLA_ROPE // 2

HEAD_DIM = 64
N_HEADS = 8
N_KV = 2
GROUP = N_HEADS // N_KV
SWA_WINDOW = 128
NSA_WINDOW = 512
CMP_LEN = 32
CMP_STRIDE = 16
CMP_HIDDEN = 256
SEL_LEN = 64
SEL_TOP_N = 16
MEM_HEADS = 4
MEM_HEAD_DIM = 128
MEM_W = MEM_HEADS * MEM_HEAD_DIM

LANES = 128
BF16_ROWS = 16
ACC_ROWS = HEAD_DIM + BF16_ROWS
BAND_TQ = 128
CMP_TQ = 512
MLA_TQ = 512
MLA_TK = 512
MLA_HEADS_PER_STEP = 2
MLA_CHUNK = 512
SLC_TQ = 256
SLC_TK = 512
MEM_TQ = 1024
PROJ_TM = 512
VMEM_LIMIT = 56 * 1024 * 1024


def _t5_far_distance():
    max_exact = NUM_BUCKETS // 2
    d = np.arange(0, 2 * REL_MAX_DIST)
    x = (np.log(np.maximum(d, 1).astype(np.float64) / max_exact)
         / math.log(REL_MAX_DIST / max_exact) * (NUM_BUCKETS - max_exact))
    assert np.abs(x - np.round(x))[max_exact + 1:REL_MAX_DIST].min() > 1e-3
    bucket = np.where(d < max_exact, d, np.minimum(max_exact + np.trunc(x).astype(np.int64), NUM_BUCKETS - 1))
    assert (np.diff(bucket) >= 0).all()
    return int(np.argmax(bucket >= NUM_BUCKETS - 1))


FAR_DIST = _t5_far_distance()

HEAD_ORDER = [h for p in range(GROUP) for h in (p, GROUP + p)]


def _cparams(sem):
    return pltpu.CompilerParams(dimension_semantics=sem, vmem_limit_bytes=VMEM_LIMIT)


def _dot(a, b):
    return jnp.dot(a, b, preferred_element_type=F32)


def _dot_nt(a, b):
    return lax.dot_general(a, b, (((1,), (1,)), ((), ())), preferred_element_type=F32)


def _rms(x, g):
    return x * lax.rsqrt(jnp.mean(x * x, axis=-1, keepdims=True) + EPS) * g


def _silu(x):
    return x * jax.nn.sigmoid(x)


def _t5_bucket(dist):
    max_exact = NUM_BUCKETS // 2
    d = jnp.maximum(dist, 0.0)
    scale = (NUM_BUCKETS - max_exact) / math.log2(REL_MAX_DIST / max_exact)
    large = max_exact + jnp.floor(jnp.log2(jnp.maximum(d, 1.0) * (1.0 / max_exact)) * scale)
    return jnp.where(d < max_exact, d, jnp.minimum(large, NUM_BUCKETS - 1.0)).astype(jnp.int32)


def _bias_lookup(bucket, tab_ref, head):
    rows = jnp.broadcast_to(tab_ref[head:head + 1, :], (bucket.shape[0], LANES))
    return jnp.concatenate(
        [jnp.take_along_axis(rows, bucket[:, c * LANES:(c + 1) * LANES], axis=1)
         for c in range(bucket.shape[1] // LANES)], axis=1)


def _split_groups(q_ref, tq, b=0):
    first = lax.broadcasted_iota(jnp.int32, (tq, LANES), 1) < HEAD_DIM
    parts = []
    for t in range(N_KV):
        for p in range(GROUP):
            qp = q_ref[b, :, p * LANES:(p + 1) * LANES]
            parts.append(jnp.where(first if t == 0 else ~first, qp, jnp.zeros_like(qp)))
    return jnp.concatenate(parts, axis=0)


def _merge_heads(out_t):
    blocks = [jnp.concatenate([out_t[p], out_t[GROUP + p]], axis=0).T for p in range(GROUP)]
    return jnp.concatenate(blocks, axis=1)


def _band_visible(i, kl, tq, window):
    pad = kl - tq
    r = lax.broadcasted_iota(jnp.int32, (kl, tq), 0)
    c = lax.broadcasted_iota(jnp.int32, (kl, tq), 1)
    delta = c + pad - r
    return ((r + (i * tq - pad)) >= 0) & (delta >= 0) & (delta < window)


def _cmp_visible(i, n, tq):
    r = lax.broadcasted_iota(jnp.int32, (n, tq), 0)
    c = lax.broadcasted_iota(jnp.int32, (n, tq), 1)
    return (r * CMP_STRIDE + (CMP_LEN - 1)) <= (c + i * tq)


def _bias_kernel(tab_ref, pq_ref, pk_ref, o_ref, *, tq, window):
    i = pl.program_id(0)
    kl = pk_ref.shape[1]
    bucket = _t5_bucket((pq_ref[0] - pk_ref[0]).astype(F32))
    visible = _cmp_visible(i, kl, tq) if window is None else _band_visible(i, kl, tq, window)
    mask_add = jnp.where(visible, 0.0, NEG_INF)
    for h in range(N_HEADS):
        o_ref[0, :, h * tq:(h + 1) * tq] = _bias_lookup(bucket, tab_ref, h) + mask_add


def _bias_tiles(tab, pos_q, pos_k, window):
    nq, _, tq = pos_q.shape
    kl = pos_k.shape[1]
    per_tile_keys = window is not None
    return pl.pallas_call(
        functools.partial(_bias_kernel, tq=tq, window=window),
        grid=(nq,),
        in_specs=[pl.BlockSpec(tab.shape, lambda i: (0, 0)),
                  pl.BlockSpec((1, 1, tq), lambda i: (i, 0, 0)),
                  pl.BlockSpec((1, kl, 1), (lambda i: (i, 0, 0)) if per_tile_keys else (lambda i: (0, 0, 0)))],
        out_specs=pl.BlockSpec((1, kl, N_HEADS * tq), lambda i: (i, 0, 0)),
        out_shape=jax.ShapeDtypeStruct((nq, kl, N_HEADS * tq), F32),
        compiler_params=_cparams(("arbitrary",)),
        name="bias_tiles",
    )(tab, pos_q, pos_k)


def _memkv_kernel(mem_ref, g_ref, wk_ref, wvt_ref, k_ref, vt_ref):
    mn = _rms(mem_ref[0], g_ref[0]).astype(BF16)
    k_ref[0, 0] = _dot(mn, wk_ref[0]).astype(BF16)
    vt_ref[0, 0] = _dot_nt(wvt_ref[0], mn).astype(BF16)


def _mem_kv(mem, mem_norm, w_mem_kv):
    B, M, D = mem.shape
    L = mem_norm.shape[0]
    wk = w_mem_kv[:, :, :MEM_W].astype(BF16)
    wvt = jnp.swapaxes(w_mem_kv[:, :, MEM_W:], 1, 2).astype(BF16)
    return pl.pallas_call(
        _memkv_kernel,
        grid=(L, B),
        in_specs=[pl.BlockSpec((1, M, D), lambda l, b: (b, 0, 0)),
                  pl.BlockSpec((1, 1, D), lambda l, b: (l, 0, 0)),
                  pl.BlockSpec((1, D, MEM_W), lambda l, b: (l, 0, 0)),
                  pl.BlockSpec((1, MEM_W, D), lambda l, b: (l, 0, 0))],
        out_specs=[pl.BlockSpec((1, 1, M, MEM_W), lambda l, b: (l, b, 0, 0)),
                   pl.BlockSpec((1, 1, MEM_W, M), lambda l, b: (l, b, 0, 0))],
        out_shape=[jax.ShapeDtypeStruct((L, B, M, MEM_W), BF16), jax.ShapeDtypeStruct((L, B, MEM_W, M), BF16)],
        compiler_params=_cparams(("arbitrary", "arbitrary")),
        name="mem_kv",
    )(mem, mem_norm.reshape(L, 1, D), wk, wvt)


E_CQ, E_CKV, E_KR, E_KRS, E_GMLA, E_QSWA, E_KSWA, E_GSWA, E_QMEM, E_GMEM, E_END = (
    0, 256, 384, 512, 640, 1152, 1664, 1792, 2304, 2816, 3328)


def _store_lane_tiles(ref, val):
    for c in range(val.shape[1] // LANES):
        ref[c] = val[:, c * LANES:(c + 1) * LANES]


def _in_even_kernel(x_ref, gpre_ref, w_ref, wvs_ref, qn_ref, kvn_ref, wq_ref, wqs_ref, wk_ref, wv_ref, c_ref, s_ref,
                    qmla_ref, kmla_ref, vmla_ref, gmla_ref, qswa_ref, kswa_ref, vswa_ref, gswa_ref,
                    qmem_ref, gmem_ref):
    h = _rms(x_ref[...], gpre_ref[...]).astype(BF16)

    def proj(lo, hi):
        return _dot(h, w_ref[:, lo:hi])

    cos = c_ref[...]
    sin = s_ref[...]
    cos8 = jnp.concatenate([cos] * MLA_HEADS, axis=1)
    sin8 = jnp.concatenate([sin] * MLA_HEADS, axis=1)

    cq = _rms(proj(E_CQ, E_CKV), qn_ref[...]).astype(BF16)
    q = _dot(cq, wq_ref[...]) * cos8 + _dot(cq, wqs_ref[...]) * sin8
    qmla_ref[...] = (q * (LOG2E * (MLA_NOPE + MLA_ROPE) ** -0.5)).astype(BF16)

    ckv = _rms(proj(E_CKV, E_KR), kvn_ref[...]).astype(BF16)
    kpe = proj(E_KR, E_KRS) * cos + proj(E_KRS, E_GMLA) * sin
    kmla_ref[...] = (_dot(ckv, wk_ref[...]) + jnp.concatenate([kpe] * MLA_HEADS, axis=1)).astype(BF16)
    vmla_ref[0] = _dot_nt(wv_ref[...], ckv).astype(BF16)

    gmla_ref[...] = proj(E_GMLA, E_QSWA).astype(BF16)
    qswa_ref[...] = (proj(E_QSWA, E_KSWA) * (LOG2E * HEAD_DIM ** -0.5)).astype(BF16)
    kswa_ref[...] = proj(E_KSWA, E_GSWA).astype(BF16)
    _store_lane_tiles(vswa_ref, _dot_nt(wvs_ref[...], h).astype(BF16))
    gswa_ref[...] = proj(E_GSWA, E_QMEM).astype(BF16)
    qmem_ref[...] = (proj(E_QMEM, E_GMEM) * (LOG2E * MEM_HEAD_DIM ** -0.5)).astype(BF16)
    gmem_ref[...] = proj(E_GMEM, E_END).astype(BF16)


def _perm_heads(w, axis):
    w = jnp.moveaxis(w, axis, 0)
    rest = w.shape[1:]
    w = w.reshape((N_KV, GROUP, HEAD_DIM) + rest).swapaxes(0, 1).reshape((N_HEADS * HEAD_DIM,) + rest)
    return jnp.moveaxis(w, 0, axis)


def _even_weights(w_in, w_uq, w_ukv):
    D = w_in.shape[0]
    z = lambda n: jnp.zeros((D, n), F32)
    c = lambda lo, hi: w_in[:, lo:hi]
    w = jnp.concatenate([
        c(0, 256), c(256, 384),
        z(MLA_NOPE), c(384, 416), z(32),
        z(MLA_NOPE), c(400, 416), c(384, 400), z(32),
        c(416, 928),
        _perm_heads(c(928, 1440), 1), c(1440, 1568),
        _perm_heads(c(1696, 2208), 1),
        c(2208, 2720), c(2720, 3232)], axis=1).astype(BF16)
    wvs_t = c(1568, 1696).T.astype(BF16)
    hq = MLA_NOPE + MLA_ROPE
    zq = lambda n: jnp.zeros((MLA_Q_RANK, n), F32)
    wq = jnp.concatenate([jnp.concatenate([w_uq[:, h * hq:(h + 1) * hq], zq(32)], axis=1)
                          for h in range(MLA_HEADS)], axis=1).astype(BF16)
    wqs = jnp.concatenate([jnp.concatenate([zq(MLA_NOPE),
                                            w_uq[:, h * hq + MLA_NOPE + ROPE_HALF:(h + 1) * hq],
                                            w_uq[:, h * hq + MLA_NOPE:h * hq + MLA_NOPE + ROPE_HALF],
                                            zq(32)], axis=1)
                           for h in range(MLA_HEADS)], axis=1).astype(BF16)
    hk = MLA_NOPE + MLA_V
    zk = jnp.zeros((MLA_KV_RANK, LANES - MLA_NOPE), F32)
    wk = jnp.concatenate([jnp.concatenate([w_ukv[:, h * hk:h * hk + MLA_NOPE], zk], axis=1)
                          for h in range(MLA_HEADS)], axis=1).astype(BF16)
    wv_t = jnp.concatenate([w_ukv[:, h * hk + MLA_NOPE:(h + 1) * hk] for h in range(MLA_HEADS)],
                           axis=1).T.astype(BF16)
    return w, wvs_t, wq, wqs, wk, wv_t


def _rope_tables(pos):
    inv = ROPE_BASE ** (-jnp.arange(ROPE_HALF, dtype=F32) / ROPE_HALF)
    ang = pos.astype(F32)[:, None] * inv[None, :]
    cos, sin = jnp.cos(ang), jnp.sin(ang)
    S = pos.shape[0]
    ones, z32 = jnp.ones((S, MLA_NOPE), F32), jnp.zeros((S, 32), F32)
    ctab = jnp.concatenate([ones, cos, cos, z32], axis=1)
    stab = jnp.concatenate([jnp.zeros((S, MLA_NOPE), F32), -sin, sin, z32], axis=1)
    return ctab, stab


def _full(shape):
    return pl.BlockSpec(shape, lambda i: (0,) * len(shape))


def _in_even(x2, gpre, w, wvs_t, qn, kvn, wq, wqs, wk, wv, ctab, stab):
    T, D = x2.shape
    S = ctab.shape[0]
    tm = PROJ_TM
    ns = S // tm
    row = lambda n: pl.BlockSpec((tm, n), lambda i: (i, 0))
    widths = [1024, 1024, 512, 512, 512, 128, 128, 512, 512, 512]
    out_specs = [row(n) for n in widths]
    out_shape = [jax.ShapeDtypeStruct((T, n), BF16) for n in widths]
    out_specs[2] = pl.BlockSpec((1, widths[2], tm), lambda i: (i, 0, 0))
    out_shape[2] = jax.ShapeDtypeStruct((T // tm, widths[2], tm), BF16)
    out_specs[6] = pl.BlockSpec((tm // LANES, LANES, LANES), lambda i: (i, 0, 0))
    out_shape[6] = jax.ShapeDtypeStruct((T // LANES, LANES, LANES), BF16)
    return pl.pallas_call(
        _in_even_kernel,
        grid=(T // tm,),
        in_specs=[row(D), _full((1, D)), _full(w.shape), _full(wvs_t.shape), _full((1, MLA_Q_RANK)),
                  _full((1, MLA_KV_RANK)), _full(wq.shape), _full(wqs.shape), _full(wk.shape), _full(wv.shape),
                  pl.BlockSpec((tm, LANES), lambda i: (i % ns, 0)),
                  pl.BlockSpec((tm, LANES), lambda i: (i % ns, 0))],
        out_specs=out_specs,
        out_shape=out_shape,
        compiler_params=_cparams(("arbitrary",)),
        name="in_proj_even",
    )(x2, gpre, w, wvs_t, qn, kvn, wq, wqs, wk, wv, ctab, stab)


def _mla_kernel(q_ref, k_ref, vt_ref, o_ref, s_scr, mx_scr, m_scr, acc_scr, *, tq, tk, nvt, hg):
    i = pl.program_id(2)
    m_scr[...] = jnp.full(m_scr.shape, NEG_INF, F32)
    acc_scr[...] = jnp.zeros(acc_scr.shape, F32)
    ones = jnp.ones((BF16_ROWS, tk), BF16)
    last = (i * tq + tq - 1) // tk

    cq = min(MLA_CHUNK, tq)

    def work(score_tile, acc_tile, diag=False):
        if score_tile is not None:
            js, ss = score_tile
            start = pl.multiple_of(js * tk, tk)
        if acc_tile is not None:
            ja, sa = acc_tile
            if diag:
                key_row = lax.broadcasted_iota(jnp.int32, (tk, cq), 0)
                query_col = lax.broadcasted_iota(jnp.int32, (tk, cq), 1)
                lead = 0 if tq == tk else i * tq - ja * tk
        for t in range(hg):
            if acc_tile is not None:
                vt = jnp.concatenate([vt_ref[ja * nvt + n, t * MLA_V:(t + 1) * MLA_V, :] for n in range(nvt)], axis=1)
                vt = jnp.concatenate([vt, ones], axis=0)
            for c in range(tq // cq):
                cols = slice(c * cq, (c + 1) * cq)
                if score_tile is not None:
                    s = _dot_nt(k_ref[0, pl.ds(start, tk), t * LANES:(t + 1) * LANES],
                                q_ref[0, cols, t * LANES:(t + 1) * LANES])
                    s_scr[ss, t, :, cols] = s
                    mx_scr[ss, t, :, cols] = jnp.max(s, axis=0, keepdims=True)
                if acc_tile is not None:
                    s = s_scr[sa, t, :, cols]
                    if diag:
                        s = jnp.where(key_row <= query_col + (lead + c * cq), s, NEG_INF)
                        tile_max = jnp.max(s, axis=0, keepdims=True)
                    else:
                        tile_max = mx_scr[sa, t, :, cols]
                    m_old = m_scr[t, :, cols]
                    m_new = jnp.maximum(m_old, tile_max)
                    p = jnp.exp2(s - m_new).astype(BF16)
                    m_scr[t, :, cols] = m_new
                    acc_scr[t, :, cols] = jnp.exp2(m_old - m_new) * acc_scr[t, :, cols] + _dot(vt, p)

    def pair(jj, carry):
        work((2 * jj + 1, 1), (2 * jj, 0))
        work((2 * jj + 2, 0), (2 * jj + 1, 1))
        return carry

    work((0, 0), None)
    lax.fori_loop(0, last // 2, pair, 0)

    @pl.when(last % 2 == 0)
    def _():
        work(None, (last, 0), diag=True)

    @pl.when(last % 2 == 1)
    def _():
        work((last, 1), (last - 1, 0))
        work(None, (last, 1), diag=True)

    out_t = jnp.concatenate([acc_scr[t, :MLA_V] / acc_scr[t, MLA_V:MLA_V + 1] for t in range(hg)], axis=0)
    o_ref[0] = out_t.T.astype(BF16)


def _mla_attention(q, k, vt):
    B, S, _ = q.shape
    vtile = vt.shape[-1]
    tq = min(MLA_TQ, S)
    tk = min(MLA_TK, S)
    hg = MLA_HEADS_PER_STEP
    return pl.pallas_call(
        functools.partial(_mla_kernel, tq=tq, tk=tk, nvt=tk // vtile, hg=hg),
        grid=(B, MLA_HEADS // hg, S // tq),
        in_specs=[pl.BlockSpec((1, tq, hg * LANES), lambda b, p, i: (b, i, p)),
                  pl.BlockSpec((1, S, hg * LANES), lambda b, p, i: (b, 0, p)),
                  pl.BlockSpec((S // vtile, hg * MLA_V, vtile), lambda b, p, i: (b, p, 0))],
        out_specs=pl.BlockSpec((1, tq, hg * MLA_V), lambda b, p, i: (b, i, p)),
        out_shape=jax.ShapeDtypeStruct((B, S, MLA_HEADS * MLA_V), BF16),
        scratch_shapes=[pltpu.VMEM((2, hg, tk, tq), F32), pltpu.VMEM((2, hg, 1, tq), F32),
                        pltpu.VMEM((hg, 1, tq), F32), pltpu.VMEM((hg, ACC_ROWS, tq), F32)],
        compiler_params=_cparams(("arbitrary", "arbitrary", "arbitrary")),
        name="mla_flash",
    )(q, k, vt)


def _band_kernel(*refs, nkb, use_sink, tq, nb):
    refs = list(refs)
    sink_ref = refs.pop(0) if use_sink else None
    q_ref = refs.pop(0)
    k_refs = [refs.pop(0) for _ in range(nkb)]
    vt_refs = [refs.pop(0) for _ in range(nkb)]
    bias_ref, o_ref = refs
    kl = nkb * tq
    ones = jnp.ones((BF16_ROWS, kl), BF16)

    def scores(bb):
        k = jnp.concatenate([r[bb] for r in k_refs], axis=0)
        return _dot_nt(k, _split_groups(q_ref, tq, bb)) + bias_ref[0]

    def finish(bb, s):
        vt = jnp.concatenate([r[bb, 0] for r in vt_refs], axis=1)
        out_t = []
        for t in range(N_KV):
            ps, ms = [], []
            for p in range(GROUP):
                h = t * GROUP + p
                sp = s[:, h * tq:(h + 1) * tq]
                m = jnp.max(sp, axis=0, keepdims=True)
                if use_sink:
                    m = jnp.maximum(m, sink_ref[h])
                ps.append(jnp.exp2(sp - m).astype(BF16))
                ms.append(m)
            vg = jnp.concatenate([vt[t * HEAD_DIM:(t + 1) * HEAD_DIM, :], ones], axis=0)
            acc = _dot(vg, jnp.concatenate(ps, axis=1))
            for p in range(GROUP):
                den = acc[HEAD_DIM:HEAD_DIM + 1, p * tq:(p + 1) * tq]
                if use_sink:
                    den = den + jnp.exp2(sink_ref[t * GROUP + p] - ms[p])
                out_t.append(acc[:HEAD_DIM, p * tq:(p + 1) * tq] / den)
        o_ref[bb] = _merge_heads(out_t).astype(BF16)

    s = scores(0)
    for bb in range(nb):
        s_next = scores(bb + 1) if bb + 1 < nb else None
        finish(bb, s)
        s = s_next


def _band_attention(q, k, vt, bias, window, sinks=None):
    B, S, W = q.shape
    tq = BAND_TQ
    nkb = -(-(window - 1) // tq) + 1
    nt = S // tq
    nb = next(n for n in (8, 4, 2, 1) if B % n == 0)
    use_sink = sinks is not None
    vt = vt.reshape(B, nt, LANES, tq)

    def k_spec(back):
        return pl.BlockSpec((nb, tq, LANES), lambda i, b: (b, jnp.maximum(i - back, 0), 0))

    def vt_spec(back):
        return pl.BlockSpec((nb, 1, LANES, tq), lambda i, b: (b, jnp.maximum(i - back, 0), 0, 0))

    in_specs = ([pl.BlockSpec(memory_space=pltpu.SMEM)] if use_sink else []) + (
        [pl.BlockSpec((nb, tq, W), lambda i, b: (b, i, 0))]
        + [k_spec(nkb - 1 - n) for n in range(nkb)] + [vt_spec(nkb - 1 - n) for n in range(nkb)]
        + [pl.BlockSpec((1, nkb * tq, N_HEADS * tq), lambda i, b: (i, 0, 0))])
    args = ([sinks] if use_sink else []) + [q] + [k] * nkb + [vt] * nkb + [bias]
    return pl.pallas_call(
        functools.partial(_band_kernel, nkb=nkb, use_sink=use_sink, tq=tq, nb=nb),
        grid=(nt, B // nb),
        in_specs=in_specs,
        out_specs=pl.BlockSpec((nb, tq, W), lambda i, b: (b, i, 0)),
        out_shape=jax.ShapeDtypeStruct((B, S, W), BF16),
        compiler_params=_cparams(("arbitrary", "arbitrary")),
        name="band_attention_w%d" % window,
    )(*args)


def _mem_kernel(qm_ref, km_ref, vmt_ref, om_ref):
    blocks = []
    for h in range(MEM_HEADS):
        sl = slice(h * MEM_HEAD_DIM, (h + 1) * MEM_HEAD_DIM)
        sm = _dot_nt(km_ref[0, :, sl], qm_ref[0, :, sl])
        pr = jnp.exp2(sm - jnp.max(sm, axis=0, keepdims=True))
        om = _dot(vmt_ref[0, sl, :], pr.astype(BF16)) / jnp.sum(pr, axis=0, keepdims=True)
        blocks.append(om.T)
    om_ref[0] = jnp.concatenate(blocks, axis=1).astype(BF16)


def _mem_attention(qm, km, vmt):
    B, S, _ = qm.shape
    M = km.shape[1]
    tq = min(MEM_TQ, S)
    return pl.pallas_call(
        _mem_kernel,
        grid=(B, S // tq),
        in_specs=[pl.BlockSpec((1, tq, MEM_W), lambda b, i: (b, i, 0)),
                  pl.BlockSpec((1, M, MEM_W), lambda b, i: (b, 0, 0)),
                  pl.BlockSpec((1, MEM_W, M), lambda b, i: (b, 0, 0))],
        out_specs=pl.BlockSpec((1, tq, MEM_W), lambda b, i: (b, i, 0)),
        out_shape=jax.ShapeDtypeStruct((B, S, MEM_W), BF16),
        compiler_params=_cparams(("arbitrary", "arbitrary")),
        name="mem_attention",
    )(qm, km, vmt)


def _gate(o_ref, g_ref):
    return (o_ref[...].astype(F32) * _silu(g_ref[...].astype(F32))).astype(BF16)


def _out_even_kernel(x_ref, o1, g1, o2, g2, o3, g3, w_ref, gpost_ref, y_ref):
    W = o1.shape[1]
    y = (_dot(_gate(o1, g1), w_ref[0:W, :]) + _dot(_gate(o2, g2), w_ref[W:2 * W, :])
         + _dot(_gate(o3, g3), w_ref[2 * W:3 * W, :]))
    y_ref[...] = x_ref[...] + _rms(y, gpost_ref[...])


def _out_even(x2, o1, g1, o2, g2, o3, g3, w, gpost):
    T, D = x2.shape
    tm = PROJ_TM
    row = lambda n: pl.BlockSpec((tm, n), lambda i: (i, 0))
    return pl.pallas_call(
        _out_even_kernel,
        grid=(T // tm,),
        in_specs=[row(D)] + [row(512)] * 6 + [_full(w.shape), _full((1, D))],
        out_specs=row(D),
        out_shape=jax.ShapeDtypeStruct((T, D), F32),
        compiler_params=_cparams(("arbitrary",)),
        name="out_proj_even",
    )(x2, o1, g1, o2, g2, o3, g3, w, gpost)


def _out_odd_kernel(x_ref, oc, os_, ow, gl_ref, gn, om, gm, ex_ref, w_ref, gpost_ref, y_ref):
    W = oc.shape[1]
    sig = jax.nn.sigmoid(gl_ref[...])
    hi = sig.astype(BF16)
    lo = (sig - hi.astype(F32)).astype(BF16)
    mix = jnp.zeros(oc.shape, F32)
    for br, o_ref in enumerate((oc, os_, ow)):
        g = _dot(hi, ex_ref[br]) + _dot(lo, ex_ref[br])
        mix = mix + g * o_ref[...].astype(F32)
    nsa = (mix * _silu(gn[...].astype(F32))).astype(BF16)
    y = _dot(nsa, w_ref[0:W, :]) + _dot(_gate(om, gm), w_ref[W:2 * W, :])
    y_ref[...] = x_ref[...] + _rms(y, gpost_ref[...])


def _out_odd(x2, oc, os_, ow, gl, gn, om, gm, ex, w, gpost):
    T, D = x2.shape
    tm = PROJ_TM
    row = lambda n: pl.BlockSpec((tm, n), lambda i: (i, 0))
    return pl.pallas_call(
        _out_odd_kernel,
        grid=(T // tm,),
        in_specs=[row(D), row(512), row(512), row(512), row(LANES), row(512), row(512), row(512),
                  _full(ex.shape), _full(w.shape), _full((1, D))],
        out_specs=row(D),
        out_shape=jax.ShapeDtypeStruct((T, D), F32),
        compiler_params=_cparams(("arbitrary",)),
        name="out_proj_odd",
    )(x2, oc, os_, ow, gl, gn, om, gm, ex, w, gpost)


O_Q, O_KC, O_VC, O_KS, O_KW, O_GL, O_GN, O_QM, O_GM, O_END = (
    0, 512, 640, 768, 896, 1024, 1152, 1664, 2176, 2688)


def _in_odd_kernel(x_ref, gpre_ref, w_ref, wvs_ref, wvw_ref, q_ref, kc_ref, vc_ref, ks_ref, vs_ref, kw_ref, vw_ref,
                   gl_ref, gn_ref, qm_ref, gm_ref):
    h = _rms(x_ref[...], gpre_ref[...]).astype(BF16)

    def proj(lo, hi):
        return _dot(h, w_ref[:, lo:hi])

    q_ref[...] = (proj(O_Q, O_KC) * (LOG2E * HEAD_DIM ** -0.5)).astype(BF16)
    kc_ref[...] = proj(O_KC, O_VC)
    vc_ref[...] = proj(O_VC, O_KS)
    ks_ref[...] = proj(O_KS, O_KW).astype(BF16)
    vs_ref[0] = _dot_nt(wvs_ref[...], h).astype(BF16)
    kw_ref[...] = proj(O_KW, O_GL).astype(BF16)
    _store_lane_tiles(vw_ref, _dot_nt(wvw_ref[...], h).astype(BF16))
    gl_ref[...] = proj(O_GL, O_GN)
    gn_ref[...] = proj(O_GN, O_QM).astype(BF16)
    qm_ref[...] = (proj(O_QM, O_GM) * (LOG2E * MEM_HEAD_DIM ** -0.5)).astype(BF16)
    gm_ref[...] = proj(O_GM, O_END).astype(BF16)


def _odd_weights(w_in):
    D = w_in.shape[0]
    c = lambda lo, hi: w_in[:, lo:hi]
    w = jnp.concatenate([
        _perm_heads(c(0, 512), 1),
        c(512, 640), c(640, 768), c(768, 896), c(1024, 1152),
        c(1280, 1304), jnp.zeros((D, LANES - 3 * N_HEADS), F32),
        _perm_heads(c(1304, 1816), 1),
        c(1816, 2328), c(2328, 2840)], axis=1).astype(BF16)
    return w, c(896, 1024).T.astype(BF16), c(1152, 1280).T.astype(BF16)


def _in_odd(x2, gpre, w, wvs_t, wvw_t):
    T, D = x2.shape
    tm = PROJ_TM
    row = lambda n: pl.BlockSpec((tm, n), lambda i: (i, 0))
    widths = [512, 128, 128, 128, 128, 128, 128, 128, 512, 512, 512]
    dtypes = [BF16, F32, F32, BF16, BF16, BF16, BF16, F32, BF16, BF16, BF16]
    out_specs = [row(n) for n in widths]
    out_shape = [jax.ShapeDtypeStruct((T, n), dt) for n, dt in zip(widths, dtypes)]
    out_specs[4] = pl.BlockSpec((1, LANES, tm), lambda i: (i, 0, 0))
    out_shape[4] = jax.ShapeDtypeStruct((T // tm, LANES, tm), BF16)
    out_specs[6] = pl.BlockSpec((tm // LANES, LANES, LANES), lambda i: (i, 0, 0))
    out_shape[6] = jax.ShapeDtypeStruct((T // LANES, LANES, LANES), BF16)
    return pl.pallas_call(
        _in_odd_kernel,
        grid=(T // tm,),
        in_specs=[row(D), _full((1, D)), _full(w.shape), _full(wvs_t.shape), _full(wvw_t.shape)],
        out_specs=out_specs,
        out_shape=out_shape,
        compiler_params=_cparams(("arbitrary",)),
        name="in_proj_odd",
    )(x2, gpre, w, wvs_t, wvw_t)


def _compress_kernel(rk_ref, rv_ref, pek_ref, pev_ref, w1k_ref, w1v_ref, w2k_ref, w2vt_ref, kc_ref, vct_ref, *, nc):
    def hidden(r_ref, pe_ref, w1_ref):
        r = r_ref[0]
        za = _dot((r + pe_ref[0:1, :]).astype(BF16), w1_ref[0])
        zb = _dot((r + pe_ref[1:2, :]).astype(BF16), w1_ref[1])
        return _silu(za + pltpu.roll(zb, r.shape[0] - 1, 0)).astype(BF16)

    kc = _dot(hidden(rk_ref, pek_ref, w1k_ref), w2k_ref[...])
    kc_ref[0] = jnp.where(lax.broadcasted_iota(jnp.int32, kc.shape, 0) < nc, kc, 0.0).astype(BF16)
    vct = _dot_nt(w2vt_ref[...], hidden(rv_ref, pev_ref, w1v_ref))
    vct_ref[0] = jnp.where(lax.broadcasted_iota(jnp.int32, vct.shape, 1) < nc, vct, 0.0).astype(BF16)


def _compress_weights(pe, w1, w2):
    eye = jnp.eye(N_KV, dtype=F32)
    d = HEAD_DIM
    w1h = w1.reshape(2, CMP_STRIDE, d, CMP_HIDDEN)
    w1x = jnp.einsum('ztdn,gh->ztgdhn', w1h, eye).reshape(2, CMP_STRIDE * N_KV * d, N_KV * CMP_HIDDEN)
    w2x = jnp.einsum('kn,gh->gkhn', w2, eye).reshape(N_KV * CMP_HIDDEN, N_KV * d)
    pex = jnp.broadcast_to(pe.reshape(2, CMP_STRIDE, 1, d), (2, CMP_STRIDE, N_KV, d)).reshape(2, -1)
    return pex, w1x.astype(BF16), w2x.astype(BF16)


def _compress(rk, rv, pek, pev, w1k, w1v, w2k, w2vt):
    B, n, W = rk.shape
    spec3 = lambda a: pl.BlockSpec(a.shape, lambda b: (0,) * a.ndim)
    rspec = pl.BlockSpec((1, n, W), lambda b: (b, 0, 0))
    return pl.pallas_call(
        functools.partial(_compress_kernel, nc=n - 1),
        grid=(B,),
        in_specs=[rspec, rspec, spec3(pek), spec3(pev), spec3(w1k), spec3(w1v), spec3(w2k), spec3(w2vt)],
        out_specs=[pl.BlockSpec((1, n, LANES), lambda b: (b, 0, 0)), pl.BlockSpec((1, LANES, n), lambda b: (b, 0, 0))],
        out_shape=[jax.ShapeDtypeStruct((B, n, LANES), BF16), jax.ShapeDtypeStruct((B, LANES, n), BF16)],
        compiler_params=_cparams(("arbitrary",)),
        name="compress_blocks",
    )(rk, rv, pek, pev, w1k, w1v, w2k, w2vt)


def _cmp_kernel(q_ref, kc_ref, vct_ref, bias_ref, mt_ref, o_ref, sel_ref, *, tq, ns, top_n):
    i = pl.program_id(0)
    n = kc_ref.shape[1]
    s = _dot_nt(kc_ref[0], _split_groups(q_ref, tq)) + bias_ref[0]
    any_visible = (lax.broadcasted_iota(jnp.int32, (1, tq), 1) + i * tq) >= CMP_LEN - 1

    out_t, psums = [], []
    for t in range(N_KV):
        ps = []
        psum = jnp.zeros((n, tq), F32)
        for p in range(GROUP):
            h = t * GROUP + p
            sp = s[:, h * tq:(h + 1) * tq]
            pr = jnp.exp2(sp - jnp.max(sp, axis=0, keepdims=True))
            den = jnp.sum(pr, axis=0, keepdims=True)
            pr = pr * jnp.where(any_visible, 1.0 / den, 0.0)
            psum = psum + pr
            ps.append(pr.astype(BF16))
        acc = _dot(vct_ref[0, t * HEAD_DIM:(t + 1) * HEAD_DIM, :], jnp.concatenate(ps, axis=1))
        out_t += [acc[:, p * tq:(p + 1) * tq] for p in range(GROUP)]
        psums.append(psum)
    o_ref[0] = _merge_heads(out_t).astype(BF16)

    gq = N_KV * tq
    blk = lax.broadcasted_iota(jnp.int32, (LANES, gq), 0)
    col = lax.broadcasted_iota(jnp.int32, (LANES, gq), 1)
    q_idx = jnp.where(col < tq, col, col - tq) + i * tq
    cur = q_idx // SEL_LEN
    forced = (blk == 0) | (blk == cur) | (blk == cur - 1)
    causal = blk * SEL_LEN <= q_idx
    psum = jnp.concatenate(psums, axis=1)
    hi = psum.astype(BF16)
    lo = (psum - hi.astype(F32)).astype(BF16)
    imp = _dot(mt_ref[...], hi) + _dot(mt_ref[...], lo)
    score = jnp.where(forced, SEL_FORCE, jnp.where(causal, imp, -SEL_FORCE))
    score = jnp.where(blk < ns, score, LOWEST)
    for _ in range(top_n):
        mx = jnp.max(score, axis=0, keepdims=True)
        idx = jnp.min(jnp.where(score == mx, blk, LANES), axis=0, keepdims=True)
        score = jnp.where(blk == idx, LOWEST, score)
    sel = jnp.where(causal & (score == LOWEST) & (blk < ns), 1.0, 0.0)
    for t in range(N_KV):
        sel_ref[0, t] = sel[:, t * tq:(t + 1) * tq].T.astype(BF16)


def _cmp_to_sel_t(n, ns):
    c_start = np.arange(n)[None, :] * CMP_STRIDE
    s_start = np.arange(LANES)[:, None] * SEL_LEN
    overlap = np.maximum(np.minimum(c_start + CMP_LEN, s_start + SEL_LEN) - np.maximum(c_start, s_start), 0)
    m = overlap.astype(np.float32) / CMP_LEN
    m[:, n - 1:] = 0.0
    m[ns:, :] = 0.0
    return jnp.asarray(m, BF16)


def _cmp_select(q, kc, vct, bias, mt):
    B, S, W = q.shape
    n = kc.shape[1]
    tq = bias.shape[2] // N_HEADS
    ns = S // SEL_LEN
    return pl.pallas_call(
        functools.partial(_cmp_kernel, tq=tq, ns=ns, top_n=min(SEL_TOP_N, ns)),
        grid=(S // tq, B),
        in_specs=[pl.BlockSpec((1, tq, W), lambda i, b: (b, i, 0)),
                  pl.BlockSpec((1, n, LANES), lambda i, b: (b, 0, 0)),
                  pl.BlockSpec((1, LANES, n), lambda i, b: (b, 0, 0)),
                  pl.BlockSpec((1, n, N_HEADS * tq), lambda i, b: (i, 0, 0)),
                  pl.BlockSpec(mt.shape, lambda i, b: (0, 0))],
        out_specs=[pl.BlockSpec((1, tq, W), lambda i, b: (b, i, 0)),
                   pl.BlockSpec((1, N_KV, tq, LANES), lambda i, b: (b, 0, i, 0))],
        out_shape=[jax.ShapeDtypeStruct((B, S, W), BF16),
                   jax.ShapeDtypeStruct((B, N_KV, S, LANES), BF16)],
        compiler_params=_cparams(("arbitrary", "arbitrary")),
        name="cmp_select",
    )(q, kc, vct, bias, mt)


def _slc_kernel(qmin_ref, kmax_ref, qmin_sub_ref, kmax_sub_ref, tab_ref, q_ref, sel_ref, pq_ref, pk_ref, k_ref,
                vt_ref, o_ref,
                qs_scr, s_scr, mx_scr, m_scr, acc_scr, *, tq, tk, nvt):
    i = pl.program_id(1)
    m_scr[...] = jnp.full(m_scr.shape, NEG_INF, F32)
    acc_scr[...] = jnp.zeros(acc_scr.shape, F32)
    qs_scr[:, :LANES] = _split_groups(q_ref, tq)
    for t in range(N_KV):
        off = jnp.where(sel_ref[0, t].astype(F32) > 0.5, 0.0, NEG_INF).astype(BF16)
        for p in range(GROUP):
            h = t * GROUP + p
            qs_scr[h * tq:(h + 1) * tq, LANES:] = off
    ones = jnp.ones((BF16_ROWS, tk), BF16)
    gq = GROUP * tq
    last = (i * tq + tq - 1) // tk

    def work(score_tile, acc_tile, near=False, diag=False):
        if score_tile is not None:
            js, ss = score_tile
            start = pl.multiple_of(js * tk, tk)
            blk = lax.broadcasted_iota(jnp.int32, (tk, LANES), 0) // SEL_LEN + js * (tk // SEL_LEN)
            onehot = jnp.where(lax.broadcasted_iota(jnp.int32, (tk, LANES), 1) == blk, 1.0, 0.0).astype(BF16)
            kx = jnp.concatenate([k_ref[0, pl.ds(start, tk), :], onehot], axis=1)
        if acc_tile is not None:
            ja, sa = acc_tile
            if diag:
                tok_ok = (lax.broadcasted_iota(jnp.int32, (tk, tq), 0)
                          - lax.broadcasted_iota(jnp.int32, (tk, tq), 1)) <= i * tq - ja * tk
        for t in range(N_KV):
            if acc_tile is not None:
                vt = jnp.concatenate([vt_ref[ja * nvt + n, t * HEAD_DIM:(t + 1) * HEAD_DIM, :] for n in range(nvt)],
                                     axis=1)
                vt = jnp.concatenate([vt, ones], axis=0)
            for p in range(GROUP):
                h = t * GROUP + p
                cols = slice(p * tq, (p + 1) * tq)
                if score_tile is not None:
                    s = _dot_nt(kx, qs_scr[h * tq:(h + 1) * tq, :])
                    s_scr[ss, t, :, cols] = s
                    mx_scr[ss, t, :, cols] = jnp.max(s, axis=0, keepdims=True)
                if acc_tile is not None:
                    sp = s_scr[sa, t, :, cols]
                    if diag:
                        sp = jnp.where(tok_ok, sp, NEG_INF)
                    tile_max = jnp.max(sp, axis=0, keepdims=True) if near or diag else mx_scr[sa, t, :, cols]
                    m_old = m_scr[t, :, cols]
                    m_new = jnp.maximum(m_old, tile_max)
                    pr = jnp.exp2(sp - m_new).astype(BF16)
                    m_scr[t, :, cols] = m_new
                    acc_scr[t, :, cols] = jnp.exp2(m_old - m_new) * acc_scr[t, :, cols] + _dot(vt, pr)

    def add_bias(j, slot):
        for kb in range(tk // LANES):
            for qb in range(tq // LANES):
                near_sub = (qmin_sub_ref[i * (tq // LANES) + qb] - kmax_sub_ref[j * (tk // LANES) + kb]) < FAR_DIST

                @pl.when(near_sub)
                def _(kb=kb, qb=qb):
                    rows = slice(kb * LANES, (kb + 1) * LANES)
                    dist = pq_ref[0, qb * LANES:(qb + 1) * LANES, :] - pk_ref[j, :, rows]
                    bucket = _t5_bucket(dist.astype(F32).T)
                    for h in range(N_HEADS):
                        t, p = divmod(h, GROUP)
                        cols = slice(p * tq + qb * LANES, p * tq + (qb + 1) * LANES)
                        s_scr[slot, t, rows, cols] = s_scr[slot, t, rows, cols] + _bias_lookup(bucket, tab_ref, h)

    def step(j_next, slot_next, j, slot):
        far = (qmin_ref[i] - kmax_ref[j]) >= FAR_DIST

        @pl.when(far)
        def _():
            work((j_next, slot_next), (j, slot))

        @pl.when(jnp.logical_not(far))
        def _():
            add_bias(j, slot)

        @pl.when(jnp.logical_not(far))
        def _():
            work((j_next, slot_next), (j, slot), near=True)

    def pair(jj, carry):
        both_far = (qmin_ref[i] - jnp.maximum(kmax_ref[2 * jj], kmax_ref[2 * jj + 1])) >= FAR_DIST

        @pl.when(both_far)
        def _():
            work((2 * jj + 1, 1), (2 * jj, 0))
            work((2 * jj + 2, 0), (2 * jj + 1, 1))

        @pl.when(jnp.logical_not(both_far))
        def _():
            step(2 * jj + 1, 1, 2 * jj, 0)
            step(2 * jj + 2, 0, 2 * jj + 1, 1)

        return carry

    work((0, 0), None)
    lax.fori_loop(0, last // 2, pair, 0)

    @pl.when(last % 2 == 1)
    def _():
        step(last, 1, last - 1, 0)

    for slot in range(2):
        @pl.when(last % 2 == slot)
        def _(slot=slot):
            add_bias(last, slot)

        @pl.when(last % 2 == slot)
        def _(slot=slot):
            work(None, (last, slot), near=True, diag=True)

    out_t = []
    for t in range(N_KV):
        for p in range(GROUP):
            cols = slice(p * tq, (p + 1) * tq)
            out_t.append(acc_scr[t, :HEAD_DIM, cols] / acc_scr[t, HEAD_DIM:HEAD_DIM + 1, cols])
    o_ref[0] = _merge_heads(out_t).astype(BF16)


def _slc_attention(q, sel, k, vt, pos, tab_rel):
    B, S, W = q.shape
    tk = min(SLC_TK, S)
    tq = min(SLC_TQ, S)
    nq, nk = S // tq, S // tk
    vtile = vt.shape[-1]
    qmin_sub = pos.reshape(S // LANES, LANES).min(axis=1)
    kmax_sub = pos.reshape(S // LANES, LANES).max(axis=1)
    qmin = qmin_sub.reshape(nq, tq // LANES).min(axis=1)
    kmax = kmax_sub.reshape(nk, tk // LANES).max(axis=1)
    smem = pl.BlockSpec(memory_space=pltpu.SMEM)
    return pl.pallas_call(
        functools.partial(_slc_kernel, tq=tq, tk=tk, nvt=tk // vtile),
        grid=(B, nq),
        in_specs=[smem, smem, smem, smem,
                  pl.BlockSpec(tab_rel.shape, lambda b, i: (0, 0)),
                  pl.BlockSpec((1, tq, W), lambda b, i: (b, i, 0)),
                  pl.BlockSpec((1, N_KV, tq, LANES), lambda b, i: (b, 0, i, 0)),
                  pl.BlockSpec((1, tq, 1), lambda b, i: (i, 0, 0)),
                  pl.BlockSpec((nk, 1, tk), lambda b, i: (0, 0, 0)),
                  pl.BlockSpec((1, S, LANES), lambda b, i: (b, 0, 0)),
                  pl.BlockSpec((S // vtile, LANES, vtile), lambda b, i: (b, 0, 0))],
        out_specs=pl.BlockSpec((1, tq, W), lambda b, i: (b, i, 0)),
        out_shape=jax.ShapeDtypeStruct((B, S, W), BF16),
        scratch_shapes=[pltpu.VMEM((N_HEADS * tq, 2 * LANES), BF16),
                        pltpu.VMEM((2, N_KV, tk, GROUP * tq), F32),
                        pltpu.VMEM((2, N_KV, 1, GROUP * tq), F32),
                        pltpu.VMEM((N_KV, 1, GROUP * tq), F32),
                        pltpu.VMEM((N_KV, ACC_ROWS, GROUP * tq), F32)],
        compiler_params=_cparams(("arbitrary", "arbitrary")),
        name="slc_attention",
    )(qmin, kmax, qmin_sub, kmax_sub, tab_rel, q, sel, pos.reshape(nq, tq, 1), pos.reshape(nk, 1, tk), k, vt)


def _band_positions(pos, tq, pad):
    S = pos.shape[0]
    nq = S // tq
    rows = jnp.pad(pos, (pad, 0)).reshape(nq + pad // tq, tq)
    win = jnp.concatenate([rows[n:n + nq] for n in range(pad // tq + 1)], axis=1)
    return pos.reshape(nq, 1, tq), win.reshape(nq, pad + tq, 1)


def _bias_table(table):
    t = jnp.zeros((N_HEADS, LANES), F32)
    return t.at[:, :NUM_BUCKETS].set(table.astype(F32).T * LOG2E)


def _gate_expand():
    ex = np.zeros((3, LANES, N_HEADS * HEAD_DIM), np.float32)
    for slot, h in enumerate(HEAD_ORDER):
        for br in range(3):
            ex[br, 3 * h + br, slot * HEAD_DIM:(slot + 1) * HEAD_DIM] = 1.0
    return jnp.asarray(ex, BF16)


def kernel(x, mem, positions, rel_bias_table, norm_pre, norm_post, mem_norm, w_mem_kv, w_in_even, mla_q_norm,
           mla_kv_norm, mla_w_uq, mla_w_ukv, swa_sinks, w_out_even, w_in_odd, cmp_pos_k, cmp_pos_v, cmp_w1_k,
           cmp_w2_k, cmp_w1_v, cmp_w2_v, w_out_odd):
    B, S, D = x.shape
    depth = norm_pre.shape[0]
    T = B * S
    pos = positions.astype(jnp.int32)
    n_cmp = S // CMP_STRIDE

    tab = _bias_table(rel_bias_table)
    tab_rel = _bias_table(rel_bias_table - rel_bias_table[NUM_BUCKETS - 1:, :])
    pq, pk_swa = _band_positions(pos, BAND_TQ, BAND_TQ)
    bias_swa = _bias_tiles(tab, pq, pk_swa, SWA_WINDOW)
    _, pk_win = _band_positions(pos, BAND_TQ, (-(-(NSA_WINDOW - 1) // BAND_TQ)) * BAND_TQ)
    bias_win = _bias_tiles(tab, pq, pk_win, NSA_WINDOW)
    pos_cmp = jnp.concatenate([pos[CMP_LEN - 1::CMP_STRIDE], pos[-1:]])
    cmp_tq = min(CMP_TQ, S)
    bias_cmp = _bias_tiles(tab, pos.reshape(S // cmp_tq, 1, cmp_tq), pos_cmp.reshape(1, n_cmp, 1), None)

    ctab, stab = _rope_tables(pos)
    mem_k, mem_vt = _mem_kv(mem, mem_norm, w_mem_kv)
    m_sel_t = _cmp_to_sel_t(n_cmp, S // SEL_LEN)
    gate_ex = _gate_expand()

    x2 = x.reshape(T, D)
    r3 = lambda a: a.reshape(B, S, a.shape[-1])
    for l in range(depth):
        gpre = norm_pre[l].reshape(1, D)
        gpost = norm_post[l].reshape(1, D)
        if l % 2 == 0:
            e = l // 2
            w, wvs_t, wq, wqs, wk, wv = _even_weights(w_in_even[e], mla_w_uq[e], mla_w_ukv[e])
            (qmla, kmla, vmla_t, gmla, qswa, kswa, vswa_t, gswa, qmem, gmem) = _in_even(
                x2, gpre, w, wvs_t, mla_q_norm[e].reshape(1, -1), mla_kv_norm[e].reshape(1, -1),
                wq, wqs, wk, wv, ctab, stab)
            o_mla = _mla_attention(r3(qmla), r3(kmla), vmla_t)
            o_swa = _band_attention(r3(qswa), r3(kswa), vswa_t, bias_swa, SWA_WINDOW, sinks=swa_sinks[e] * LOG2E)
            o_mem = _mem_attention(r3(qmem), mem_k[l], mem_vt[l])
            w_o = w_out_even[e]
            w_o = jnp.concatenate([w_o[:512], _perm_heads(w_o[512:1024], 0), w_o[1024:]], axis=0).astype(BF16)
            x2 = _out_even(x2, o_mla.reshape(T, -1), gmla, o_swa.reshape(T, -1), gswa,
                           o_mem.reshape(T, -1), gmem, w_o, gpost)
        else:
            o = l // 2
            w, wvs_t, wvw_t = _odd_weights(w_in_odd[o])
            (q, kc_in, vc_in, ks, vs_t, kw, vw_t, gl, gn, qmem, gmem) = _in_odd(x2, gpre, w, wvs_t, wvw_t)
            pek, w1k, w2k = _compress_weights(cmp_pos_k[o], cmp_w1_k[o], cmp_w2_k[o])
            pev, w1v, w2v = _compress_weights(cmp_pos_v[o], cmp_w1_v[o], cmp_w2_v[o])
            chunks = lambda a: a.reshape(B, n_cmp, CMP_STRIDE * LANES)
            kc, vc_t = _compress(chunks(kc_in), chunks(vc_in), pek, pev, w1k, w1v, w2k, w2v.T)
            o_cmp, sel = _cmp_select(r3(q), kc, vc_t, bias_cmp, m_sel_t)
            o_slc = _slc_attention(r3(q), sel, r3(ks), vs_t, pos, tab_rel)
            o_win = _band_attention(r3(q), r3(kw), vw_t, bias_win, NSA_WINDOW)
            o_mem = _mem_attention(r3(qmem), mem_k[l], mem_vt[l])
            w_o = w_out_odd[o]
            w_o = jnp.concatenate([_perm_heads(w_o[:512], 0), w_o[512:]], axis=0).astype(BF16)
            x2 = _out_odd(x2, o_cmp.reshape(T, -1), o_slc.reshape(T, -1), o_win.reshape(T, -1), gl, gn,
                          o_mem.reshape(T, -1), gmem, gate_ex, w_o, gpost)
    return x2.reshape(B, S, D)
```
